```python
import jax, jax.numpy as jnp
from jax import lax
import numpy as np

D_MODEL = 1024
BATCH = 16
SEQ = 4096
DEPTH = 2

D_PLE = 256
D_SSM = 256
D_DN = 512
D_SG = 256
D_MIX = D_SSM + D_DN + D_SG
SSM_GROUP = 16
SSM_GROUPS = D_SSM // SSM_GROUP
SSM_STATE = 64
DN_HEADS = 4
DN_HEAD_DIM = D_DN // DN_HEADS
DN_CONV = 4
DN_CHUNK = 64
SG_HEADS = 4
SG_HEAD_DIM = D_SG // SG_HEADS
SG_CHUNK = 128
EPS = 1e-6
SPLITS = (D_SSM, D_SSM, 3 * D_DN, DN_HEADS, DN_HEADS, D_DN, D_SG, D_SG, D_SG)
D_IN = 2 * D_SSM + 4 * D_DN + 2 * DN_HEADS + 3 * D_SG

kernel_name = "hymba_style_s5_gdn_gmlp_ple"


def rms_norm(x, g):
    xf = x.astype(jnp.float32)
    y = xf * lax.rsqrt(jnp.mean(xf * xf, axis=-1, keepdims=True) + EPS)
    return (y * g.astype(jnp.float32)).astype(x.dtype)


def layer_norm(x, g, b):
    xf = x.astype(jnp.float32)
    mu = jnp.mean(xf, axis=-1, keepdims=True)
    xc = xf - mu
    y = xc * lax.rsqrt(jnp.mean(xc * xc, axis=-1, keepdims=True) + EPS)
    return (y * g.astype(jnp.float32) + b.astype(jnp.float32)).astype(x.dtype)


def l2_normalize(x):
    return x * lax.rsqrt(jnp.sum(x * x, axis=-1, keepdims=True) + EPS)


def split_cols(z):
    idx = np.cumsum(np.array(SPLITS))[:-1].tolist()
    return jnp.split(z, idx, axis=-1)


def complex_linear_combine(e1, e2):
    a1r, a1i, b1r, b1i = e1
    a2r, a2i, b2r, b2i = e2
    ar = a2r * a1r - a2i * a1i
    ai = a2r * a1i + a2i * a1r
    br = a2r * b1r - a2i * b1i + b2r
    bi = a2r * b1i + a2i * b1r + b2i
    return (ar, ai, br, bi)


def s5_branch(u, a_re, a_im, b_re, b_im, c_re, c_im, d_skip, log_step, w_glu, b_glu):
    bsz, seq, _ = u.shape
    f32 = jnp.float32
    uf = u.astype(f32).reshape(bsz, seq, SSM_GROUPS, SSM_GROUP)
    step = jnp.exp(log_step.astype(f32))[:, None]
    ar, ai = a_re.astype(f32), a_im.astype(f32)
    mag = jnp.exp(ar * step)
    lam_re = mag * jnp.cos(ai * step)
    lam_im = mag * jnp.sin(ai * step)
    den = ar * ar + ai * ai
    nr, ni = lam_re - 1.0, lam_im
    f_re = (nr * ar + ni * ai) / den
    f_im = (ni * ar - nr * ai) / den
    br, bi = b_re.astype(f32), b_im.astype(f32)
    bbar_re = f_re[..., None] * br - f_im[..., None] * bi
    bbar_im = f_re[..., None] * bi + f_im[..., None] * br
    bu_re = jnp.einsum('bsgc,gnc->bsgn', uf, bbar_re)
    bu_im = jnp.einsum('bsgc,gnc->bsgn', uf, bbar_im)
    lam_re_s = jnp.broadcast_to(lam_re, (1, seq, SSM_GROUPS, SSM_STATE))
    lam_im_s = jnp.broadcast_to(lam_im, (1, seq, SSM_GROUPS, SSM_STATE))
    _, _, h_re, h_im = lax.associative_scan(
        complex_linear_combine, (lam_re_s, lam_im_s, bu_re, bu_im), axis=1)
    y = (jnp.einsum('bsgn,gcn->bsgc', h_re, c_re.astype(f32))
         - jnp.einsum('bsgn,gcn->bsgc', h_im, c_im.astype(f32))
         + d_skip.astype(f32) * uf)
    y = jax.nn.gelu(y.reshape(bsz, seq, D_SSM))
    y = y * jax.nn.sigmoid(jnp.einsum('bse,ef->bsf', y, w_glu.astype(f32)) + b_glu.astype(f32))
    return y.astype(u.dtype)


def causal_depthwise_conv(x, w):
    k_len, ch = w.shape
    return lax.conv_general_dilated(
        x, w[:, None, :], window_strides=(1,), padding=[(k_len - 1, 0)],
        dimension_numbers=('NWC', 'WIO', 'NWC'), feature_group_count=ch)


def gated_deltanet_branch(qkv, a_in, b_in, conv_w, a_log, dt_bias, norm_g):
    bsz, seq, _ = qkv.shape
    f32 = jnp.float32
    H, Dh, C = DN_HEADS, DN_HEAD_DIM, DN_CHUNK
    nc = seq // C
    qkv = jax.nn.silu(causal_depthwise_conv(qkv.astype(f32), conv_w.astype(f32)))
    q, k, v = jnp.split(qkv, 3, axis=-1)

    def to_chunks(t):
        return t.reshape(bsz, nc, C, H, Dh).transpose(0, 3, 1, 2, 4)

    def head_scalars(t):
        return t.reshape(bsz, nc, C, H).transpose(0, 3, 1, 2)

    q = l2_normalize(to_chunks(q)) * (Dh ** -0.5)
    k = l2_normalize(to_chunks(k))
    v = to_chunks(v)
    beta = head_scalars(jax.nn.sigmoid(b_in.astype(f32)))
    g = head_scalars(-jnp.exp(a_log.astype(f32)) * jax.nn.softplus(a_in.astype(f32) + dt_bias.astype(f32)))
    gc = jnp.cumsum(g, axis=-1)
    causal = jnp.tril(jnp.ones((C, C), dtype=bool))
    strict = jnp.tril(jnp.ones((C, C), dtype=bool), k=-1)
    diff = gc[..., :, None] - gc[..., None, :]
    decay = jnp.where(causal, jnp.exp(jnp.where(causal, diff, 0.0)), 0.0)
    kb = k * beta[..., None]
    vb = v * beta[..., None]
    m = jnp.where(strict, jnp.einsum('bhnid,bhnjd->bhnij', kb, k) * decay, 0.0)
    rhs = jnp.concatenate([vb, kb * jnp.exp(gc)[..., None]], axis=-1)
    sol = lax.linalg.triangular_solve(m, rhs, left_side=True, lower=True, unit_diagonal=True)
    value, k_cd = sol[..., :Dh], sol[..., Dh:]
    attn = jnp.einsum('bhnid,bhnjd->bhnij', q, k) * decay
    q_dec = q * jnp.exp(gc)[..., None]
    k_dec = k * jnp.exp(gc[..., -1:] - gc)[..., None]
    last = jnp.exp(gc[..., -1])
    xs = (jnp.moveaxis(value, 2, 0), jnp.moveaxis(k_cd, 2, 0), jnp.moveaxis(attn, 2, 0),
          jnp.moveaxis(q_dec, 2, 0), jnp.moveaxis(k_dec, 2, 0), jnp.moveaxis(last, 2, 0))

    def chunk_step(state, inp):
        val, kcd, att, qd, kd, dl = inp
        v_new = val - jnp.einsum('bhcd,bhde->bhce', kcd, state)
        o = jnp.einsum('bhcd,bhde->bhce', qd, state) + jnp.einsum('bhij,bhje->bhie', att, v_new)
        state = state * dl[..., None, None] + jnp.einsum('bhcd,bhce->bhde', kd, v_new)
        return state, o

    s0 = jnp.zeros((bsz, H, Dh, Dh), f32)
    _, o = lax.scan(chunk_step, s0, xs)
    o = o.transpose(1, 0, 3, 2, 4).reshape(bsz, seq, H, Dh)
    o = rms_norm(o, norm_g)
    return o.reshape(bsz, seq, D_DN).astype(a_in.dtype)


def spatial_gating_branch(u, v, ln_g, ln_b, w_sp, b_sp):
    bsz, seq, _ = u.shape
    nch = seq // SG_CHUNK
    u = jax.nn.gelu(u)
    v = layer_norm(jax.nn.gelu(v), ln_g, ln_b)
    vh = v.reshape(bsz, nch, SG_CHUNK, SG_HEADS, SG_HEAD_DIM)
    causal = jnp.tril(jnp.ones((SG_CHUNK, SG_CHUNK), dtype=bool))
    w = jnp.where(causal, w_sp, 0.0)
    s = jnp.einsum('hts,bnshc->bnthc', w, vh) + jnp.transpose(b_sp)[:, :, None]
    return u * s.reshape(bsz, seq, D_SG)


def setup_inputs(seed: int = 0) -> dict:
    key = jax.random.key(seed)
    ks = jax.random.split(key, 32)
    f32 = jnp.float32
    L, D = DEPTH, D_MODEL
    G, N, Cg = SSM_GROUPS, SSM_STATE, SSM_GROUP

    def nrm(k, shape, scale):
        return scale * jax.random.normal(k, shape, f32)

    x = jax.random.normal(ks[0], (BATCH, SEQ, D), f32)
    p = jax.random.normal(ks[1], (DEPTH, BATCH, SEQ, D_PLE), f32)
    norm_g = 1.0 + nrm(ks[2], (L, D), 0.02)
    w_in = nrm(ks[3], (L, D, D_IN), D ** -0.5)
    ssm_a_re = -0.5 + nrm(ks[4], (L, G, N), 0.01)
    ssm_a_im = jnp.pi * jnp.arange(N, dtype=f32) + nrm(ks[5], (L, G, N), 0.01)
    ssm_b_re = nrm(ks[6], (L, G, N, Cg), Cg ** -0.5)
    ssm_b_im = nrm(ks[7], (L, G, N, Cg), Cg ** -0.5)
    ssm_c_re = nrm(ks[8], (L, G, Cg, N), N ** -0.5)
    ssm_c_im = nrm(ks[9], (L, G, Cg, N), N ** -0.5)
    ssm_d = nrm(ks[10], (L, G, Cg), 1.0)
    ssm_log_step = jax.random.uniform(ks[11], (L, G), f32, np.log(1e-3), np.log(1e-1))
    ssm_w_glu = nrm(ks[12], (L, D_SSM, D_SSM), D_SSM ** -0.5)
    ssm_b_glu = nrm(ks[13], (L, D_SSM), 0.01)
    dn_conv_w = nrm(ks[14], (L, DN_CONV, 3 * D_DN), DN_CONV ** -0.5)
    dn_a_log = jnp.log(jax.random.uniform(ks[15], (L, DN_HEADS), f32, 1.0, 16.0))
    dt = jnp.exp(jax.random.uniform(ks[16], (L, DN_HEADS), f32, np.log(1e-3), np.log(1e-1)))
    dn_dt_bias = dt + jnp.log(-jnp.expm1(-dt))
    dn_norm_g = 1.0 + nrm(ks[17], (L, DN_HEAD_DIM), 0.02)
    sg_ln_g = 1.0 + nrm(ks[18], (L, D_SG), 0.02)
    sg_ln_b = nrm(ks[19], (L, D_SG), 0.01)
    sg_w = nrm(ks[20], (L, SG_HEADS, SG_CHUNK, SG_CHUNK), SG_CHUNK ** -0.5)
    sg_b = 1.0 + nrm(ks[21], (L, SG_HEADS, SG_CHUNK), 0.02)
    w_out = nrm(ks[22], (L, D_MIX, D), D_MIX ** -0.5)
    ple_norm_g = 1.0 + nrm(ks[23], (L, D), 0.02)
    w_ple_gate = nrm(ks[24], (L, D, D), D ** -0.5)
    w_ple = nrm(ks[25], (L, D_PLE, D), D_PLE ** -0.5)
    final_norm_g = 1.0 + nrm(ks[26], (D,), 0.02)
    return {"x": x, "p": p, "norm_g": norm_g, "w_in": w_in,
            "ssm_a_re": ssm_a_re, "ssm_a_im": ssm_a_im, "ssm_b_re": ssm_b_re, "ssm_b_im": ssm_b_im,
            "ssm_c_re": ssm_c_re, "ssm_c_im": ssm_c_im, "ssm_d": ssm_d, "ssm_log_step": ssm_log_step,
            "ssm_w_glu": ssm_w_glu, "ssm_b_glu": ssm_b_glu,
            "dn_conv_w": dn_conv_w, "dn_a_log": dn_a_log, "dn_dt_bias": dn_dt_bias, "dn_norm_g": dn_norm_g,
            "sg_ln_g": sg_ln_g, "sg_ln_b": sg_ln_b, "sg_w": sg_w, "sg_b": sg_b,
            "w_out": w_out, "ple_norm_g": ple_norm_g, "w_ple_gate": w_ple_gate, "w_ple": w_ple,
            "final_norm_g": final_norm_g}


def reference(x, p, norm_g, w_in, ssm_a_re, ssm_a_im, ssm_b_re, ssm_b_im, ssm_c_re, ssm_c_im,
              ssm_d, ssm_log_step, ssm_w_glu, ssm_b_glu, dn_conv_w, dn_a_log, dn_dt_bias, dn_norm_g,
              sg_ln_g, sg_ln_b, sg_w, sg_b, w_out, ple_norm_g, w_ple_gate, w_ple, final_norm_g):
    for i in range(DEPTH):
        h = rms_norm(x, norm_g[i])
        z = jnp.einsum('bsd,de->bse', h, w_in[i])
        u_ssm, g_ssm, qkv, a_dn, b_dn, g_dn, u_sg, v_sg, g_sg = split_cols(z)
        y_ssm = s5_branch(u_ssm, ssm_a_re[i], ssm_a_im[i], ssm_b_re[i], ssm_b_im[i],
                          ssm_c_re[i], ssm_c_im[i], ssm_d[i], ssm_log_step[i],
                          ssm_w_glu[i], ssm_b_glu[i]) * jax.nn.silu(g_ssm)
        y_dn = gated_deltanet_branch(qkv, a_dn, b_dn, dn_conv_w[i], dn_a_log[i], dn_dt_bias[i],
                                     dn_norm_g[i]) * jax.nn.silu(g_dn)
        y_sg = spatial_gating_branch(u_sg, v_sg, sg_ln_g[i], sg_ln_b[i], sg_w[i], sg_b[i]) * jax.nn.silu(g_sg)
        y = jnp.concatenate([y_ssm, y_dn, y_sg], axis=-1)
        x = x + jnp.einsum('bse,ed->bsd', y, w_out[i])
        gate = jax.nn.sigmoid(jnp.einsum('bsd,de->bse', rms_norm(x, ple_norm_g[i]), w_ple_gate[i]))
        x = x + gate * jnp.einsum('bsk,kd->bsd', p[i], w_ple[i])
    return rms_norm(x, final_norm_g)
```

```python
import functools

import numpy as np
import jax
import jax.numpy as jnp
from jax import lax
from jax.experimental import pallas as pl
from jax.experimental.pallas import tpu as pltpu

F32 = jnp.float32
BF16 = jnp.bfloat16
HIGHEST = lax.Precision.HIGHEST

D_MODEL = 1024
D_PLE = 256
D_SSM = 256
D_DN = 512
D_SG = 256
SSM_GROUP = 16
SSM_GROUPS = 16
SSM_STATE = 64
DN_HEADS = 4
DN_HEAD_DIM = 128
DN_CONV = 4
DN_CHUNK = 64
SG_HEADS = 4
SG_HEAD_DIM = 64
SG_CHUNK = 128
EPS = 1e-6

S5_T = 16
AB_PAD = 128
W_SSM_U, W_SSM_G, W_QKV, W_AB, W_GDN, W_SG = D_SSM, D_SSM, 3 * D_DN, AB_PAD, D_DN, 3 * D_SG
IN_WIDTHS = (W_SSM_U, W_SSM_G, W_QKV, W_AB, W_GDN, W_SG)
D_IN_PAD = sum(IN_WIDTHS)

VMEM_LIMIT = 56 * 1024 * 1024


def _cparams(sem):
    return pltpu.CompilerParams(dimension_semantics=sem, vmem_limit_bytes=VMEM_LIMIT)


def _mm(a, b):
    return jnp.dot(a, b, preferred_element_type=F32)


def _mm_nt(a, b):
    return lax.dot_general(a, b, (((1,), (1,)), ((), ())), preferred_element_type=F32)


def _bdot(a, b):
    return _mm(a.astype(BF16), b.astype(BF16))


def _bdot_nt(a, b):
    return _mm_nt(a.astype(BF16), b.astype(BF16))


def _split3(a):
    hi = a.astype(BF16)
    r1 = a - hi.astype(F32)
    mid = r1.astype(BF16)
    lo = (r1 - mid.astype(F32)).astype(BF16)
    return hi, mid, lo


def _dot_x3(a, w_hi, w_lo):
    a_hi = a.astype(BF16)
    a_lo = (a - a_hi.astype(F32)).astype(BF16)
    return _mm(a_hi, w_hi) + _mm(a_lo, w_hi) + _mm(a_hi, w_lo)


def _gelu(x):
    return 0.5 * x * (1.0 + jnp.tanh(np.sqrt(2.0 / np.pi).astype(np.float32) * (x + 0.044715 * (x * x * x))))


def _sigmoid(x):
    return 1.0 / (1.0 + jnp.exp(-x))


def _silu(x):
    return x * _sigmoid(x)


def _softplus(x):
    return jnp.maximum(x, 0.0) + jnp.log1p(jnp.exp(-jnp.abs(x)))


def _rms(x, g):
    return x * lax.rsqrt(jnp.mean(x * x, axis=-1, keepdims=True) + EPS) * g


def _inproj_kernel(x_ref, g_ref, w_ref, *out_refs):
    h = _rms(x_ref[...], g_ref[...]).astype(BF16)
    off = 0
    for o_ref, width in zip(out_refs, IN_WIDTHS):
        o_ref[...] = _mm(h, w_ref[:, off:off + width])
        off += width


def _inproj(x2, norm_g, w_in_p, tm):
    n = x2.shape[0]
    outs = tuple(jax.ShapeDtypeStruct((n, w), F32) for w in IN_WIDTHS)
    return pl.pallas_call(
        _inproj_kernel,
        out_shape=outs,
        grid=(n // tm,),
        in_specs=[pl.BlockSpec((tm, D_MODEL), lambda i: (i, 0)),
                  pl.BlockSpec((1, D_MODEL), lambda i: (0, 0)),
                  pl.BlockSpec((D_MODEL, D_IN_PAD), lambda i: (0, 0))],
        out_specs=tuple(pl.BlockSpec((tm, w), lambda i: (i, 0)) for w in IN_WIDTHS),
        compiler_params=_cparams(("parallel",)),
        name="inproj",
    )(x2, norm_g.reshape(1, D_MODEL), w_in_p)


def _sgate_kernel(z_ref, lng_ref, lnb_ref, w_ref, bias_ref, o_ref, *, tm):
    z = z_ref[...]
    u = _gelu(z[:, :D_SG])
    v = _gelu(z[:, D_SG:2 * D_SG])
    gate = z[:, 2 * D_SG:]
    mu = jnp.mean(v, axis=-1, keepdims=True)
    vc = v - mu
    vn = vc * lax.rsqrt(jnp.mean(vc * vc, axis=-1, keepdims=True) + EPS) * lng_ref[...] + lnb_ref[...]
    vn = vn.astype(BF16)
    row = lax.broadcasted_iota(jnp.int32, (SG_CHUNK, SG_CHUNK), 0)
    col = lax.broadcasted_iota(jnp.int32, (SG_CHUNK, SG_CHUNK), 1)
    causal = row >= col
    head_of_lane = jnp.right_shift(lax.broadcasted_iota(jnp.int32, (SG_CHUNK, D_SG), 1), 6)
    w_heads = [jnp.where(causal, w_ref[h], 0.0).astype(BF16) for h in range(SG_HEADS)]
    bias = bias_ref[...]
    for c in range(tm // SG_CHUNK):
        rows = slice(c * SG_CHUNK, (c + 1) * SG_CHUNK)
        vch = vn[rows]
        s = bias
        for h in range(SG_HEADS):
            r = _mm(w_heads[h], vch)
            s = s + jnp.where(head_of_lane == h, r, 0.0)
        o_ref[rows, :] = u[rows] * s * _silu(gate[rows])


def _sgate(z_sg, ln_g, ln_b, w_sp, b_sp, tm):
    n = z_sg.shape[0]
    bias = jnp.repeat(jnp.transpose(b_sp), SG_HEAD_DIM, axis=1)
    return pl.pallas_call(
        functools.partial(_sgate_kernel, tm=tm),
        out_shape=jax.ShapeDtypeStruct((n, D_SG), F32),
        grid=(n // tm,),
        in_specs=[pl.BlockSpec((tm, W_SG), lambda i: (i, 0)),
                  pl.BlockSpec((1, D_SG), lambda i: (0, 0)),
                  pl.BlockSpec((1, D_SG), lambda i: (0, 0)),
                  pl.BlockSpec((SG_HEADS, SG_CHUNK, SG_CHUNK), lambda i: (0, 0, 0)),
                  pl.BlockSpec((SG_CHUNK, D_SG), lambda i: (0, 0))],
        out_specs=pl.BlockSpec((tm, D_SG), lambda i: (i, 0)),
        compiler_params=_cparams(("parallel",)),
        name="sgate",
    )(z_sg, ln_g.reshape(1, D_SG), ln_b.reshape(1, D_SG), w_sp, bias)


DN_PAIR = 2 * DN_CHUNK
DN_BLK = 16


def _unit_lower_inverse(m, eye, blk):
    d = jnp.where(blk, m, 0.0)
    l = jnp.where(blk, 0.0, m)
    d2 = _bdot(d, d)
    d4 = _bdot(d2, d2)
    d8 = _bdot(d4, d4)
    inv_d = _bdot(_bdot(_bdot(eye - d, eye + d2), eye + d4), eye + d8)
    n = _bdot(inv_d, l)
    n2 = _bdot(n, n)
    inv_n = _bdot(eye - n, eye + n2)
    return _bdot(inv_n, inv_d)


def _dnet_kernel(qkv_ref, ab_ref, gdn_ref, convw_ref, hpar_ref, normg_ref, o_ref,
                 state_ref, halo_ref, *, ts):
    H, Dh, C, P = DN_HEADS, DN_HEAD_DIM, DN_CHUNK, DN_PAIR
    npair = ts // P

    @pl.when(pl.program_id(1) == 0)
    def _():
        state_ref[...] = jnp.zeros_like(state_ref)
        halo_ref[...] = jnp.zeros_like(halo_ref)

    x = qkv_ref[...]
    w = convw_ref[...]
    xe = jnp.concatenate([halo_ref[...], x], axis=0)
    conv = (x * w[3:4] + xe[7:7 + ts] * w[2:3] + xe[6:6 + ts] * w[1:2] + xe[5:5 + ts] * w[0:1])
    halo_ref[...] = x[ts - 8:ts]
    act = _silu(conv)

    ab = ab_ref[...]
    g_all = -jnp.exp(hpar_ref[0:1, :]) * _softplus(ab + hpar_ref[1:2, :])
    beta_all = _sigmoid(ab)

    row = lax.broadcasted_iota(jnp.int32, (P, P), 0)
    col = lax.broadcasted_iota(jnp.int32, (P, P), 1)
    same_chunk = jnp.right_shift(row, 6) == jnp.right_shift(col, 6)
    causal = jnp.logical_and(same_chunk, row >= col)
    strict = jnp.logical_and(same_chunk, row > col)
    blk = jnp.right_shift(row, 4) == jnp.right_shift(col, 4)
    eye = jnp.where(row == col, 1.0, 0.0).astype(F32)
    tri = jnp.where(causal, 1.0, 0.0).astype(F32)
    first_col = col < C
    first_row1 = lax.broadcasted_iota(jnp.int32, (P, 1), 0) < C

    states = [state_ref[h] for h in range(H)]
    normg = normg_ref[...]

    pre = []
    for pp in range(npair):
        rows = slice(pp * P, (pp + 1) * P)
        g_hi, g_mid, g_lo = _split3(g_all[rows])
        tri_b = tri.astype(BF16)
        gc = _mm(tri_b, g_hi) + _mm(tri_b, g_mid) + _mm(tri_b, g_lo)
        gct = gc.T
        beta = beta_all[rows]
        per_head = []
        for h in range(H):
            q = act[rows, h * Dh:(h + 1) * Dh]
            k = act[rows, D_DN + h * Dh:D_DN + (h + 1) * Dh]
            v = act[rows, 2 * D_DN + h * Dh:2 * D_DN + (h + 1) * Dh]
            q = q * lax.rsqrt(jnp.sum(q * q, axis=-1, keepdims=True) + EPS) * (Dh ** -0.5)
            k = k * lax.rsqrt(jnp.sum(k * k, axis=-1, keepdims=True) + EPS)
            gcol = gc[:, h:h + 1]
            grow = gct[h:h + 1, :]
            bcol = beta[:, H + h:H + h + 1]
            diff = gcol - grow
            decay = jnp.where(causal, jnp.exp(jnp.where(causal, diff, 0.0)), 0.0)
            kb = k * bcol
            vb = v * bcol
            m = jnp.where(strict, _bdot_nt(kb, k) * decay, 0.0)
            tinv = _unit_lower_inverse(m, eye, blk)
            eg = jnp.exp(gcol)
            sol = _bdot(tinv, jnp.concatenate([kb * eg, vb], axis=1))
            attn = _bdot_nt(q, k) * decay
            av = _bdot(attn, sol)
            qe = q * eg - av[:, :Dh]
            oloc = av[:, Dh:]
            glast = jnp.where(first_row1, gc[C - 1:C, h:h + 1], gc[P - 1:P, h:h + 1])
            kd = k * jnp.exp(glast - gcol)
            kdt = kd.T
            gb0 = _bdot(jnp.where(first_col, kdt, 0.0), sol)
            gb1 = _bdot(jnp.where(first_col, 0.0, kdt), sol)
            last0 = jnp.exp(gc[C - 1:C, h:h + 1])
            last1 = jnp.exp(gc[P - 1:P, h:h + 1])
            per_head.append((qe, oloc, (gb0, gb1), (last0, last1)))
        pre.append(per_head)

    for pp in range(npair):
        for half in range(2):
            r0 = pp * P + half * C
            for h in range(H):
                qe, oloc, gbs, lasts = pre[pp][h]
                gb = gbs[half]
                s_mat = states[h]
                lhs = jnp.concatenate([qe[half * C:(half + 1) * C], gb[:, :Dh]], axis=0)
                r = _bdot(lhs, s_mat)
                o = r[:C] + oloc[half * C:(half + 1) * C]
                states[h] = lasts[half] * s_mat - r[C:] + gb[:, Dh:]
                on = _rms(o, normg)
                gate = gdn_ref[r0:r0 + C, h * Dh:(h + 1) * Dh]
                o_ref[r0:r0 + C, h * Dh:(h + 1) * Dh] = on * _silu(gate)

    for h in range(H):
        state_ref[h] = states[h]


def _dnet(qkv, ab, gdn, conv_w, a_log, dt_bias, norm_g, ts):
    bsz, seq, _ = qkv.shape
    hpar = jnp.zeros((8, AB_PAD), F32)
    hpar = hpar.at[0, :DN_HEADS].set(a_log.astype(F32)).at[1, :DN_HEADS].set(dt_bias.astype(F32))
    return pl.pallas_call(
        functools.partial(_dnet_kernel, ts=ts),
        out_shape=jax.ShapeDtypeStruct((bsz, seq, D_DN), F32),
        grid=(bsz, seq // ts),
        in_specs=[pl.BlockSpec((None, ts, W_QKV), lambda b, j: (b, j, 0)),
                  pl.BlockSpec((None, ts, W_AB), lambda b, j: (b, j, 0)),
                  pl.BlockSpec((None, ts, W_GDN), lambda b, j: (b, j, 0)),
                  pl.BlockSpec((DN_CONV, W_QKV), lambda b, j: (0, 0)),
                  pl.BlockSpec((8, AB_PAD), lambda b, j: (0, 0)),
                  pl.BlockSpec((1, DN_HEAD_DIM), lambda b, j: (0, 0))],
        out_specs=pl.BlockSpec((None, ts, D_DN), lambda b, j: (b, j, 0)),
        scratch_shapes=[pltpu.VMEM((DN_HEADS, DN_HEAD_DIM, DN_HEAD_DIM), F32),
                        pltpu.VMEM((8, W_QKV), F32)],
        compiler_params=_cparams(("parallel", "arbitrary")),
        name="dnet",
    )(qkv, ab, gdn, conv_w, hpar, norm_g.reshape(1, DN_HEAD_DIM))


def _s5_matrices(a_re, a_im, b_re, b_im, c_re, c_im, d_skip, log_step):
    f32 = F32
    G, N, Cg, T = SSM_GROUPS, SSM_STATE, SSM_GROUP, S5_T
    step = jnp.exp(log_step.astype(f32))[:, None]
    ar, ai = a_re.astype(f32), a_im.astype(f32)
    kk = jnp.arange(T + 1, dtype=f32)[:, None, None]
    pmag = jnp.exp(kk * (ar * step))
    p_re = pmag * jnp.cos(kk * (ai * step))
    p_im = pmag * jnp.sin(kk * (ai * step))
    lam_re, lam_im = p_re[1], p_im[1]
    den = ar * ar + ai * ai
    nr, ni = lam_re - 1.0, lam_im
    f_re = (nr * ar + ni * ai) / den
    f_im = (ni * ar - nr * ai) / den
    br, bi = b_re.astype(f32), b_im.astype(f32)
    bb_re = f_re[..., None] * br - f_im[..., None] * bi
    bb_im = f_re[..., None] * bi + f_im[..., None] * br
    cr, ci = c_re.astype(f32), c_im.astype(f32)
    hp = HIGHEST
    cp_re = cr[None] * p_re[:T, :, None, :] - ci[None] * p_im[:T, :, None, :]
    cp_im = cr[None] * p_im[:T, :, None, :] + ci[None] * p_re[:T, :, None, :]
    kmat = (jnp.einsum('tgcn,gnd->tgcd', cp_re, bb_re, precision=hp)
            - jnp.einsum('tgcn,gnd->tgcd', cp_im, bb_im, precision=hp))
    kmat = kmat.at[0].add(jnp.eye(Cg, dtype=f32)[None] * d_skip.astype(f32)[:, :, None])
    s_idx = jnp.arange(T)[:, None]
    t_idx = jnp.arange(T)[None, :]
    tau = t_idx - s_idx
    ktoe = jnp.where((tau >= 0)[:, :, None, None, None], kmat[jnp.clip(tau, 0, T - 1)], 0.0)
    m_mat = jnp.transpose(ktoe, (2, 0, 4, 1, 3)).reshape(G, T * Cg, T * Cg)
    q_re = p_re[:T][::-1]
    q_im = p_im[:T][::-1]
    bm_re = q_re[..., None] * bb_re[None] - q_im[..., None] * bb_im[None]
    bm_im = q_re[..., None] * bb_im[None] + q_im[..., None] * bb_re[None]
    bm_re = jnp.transpose(bm_re, (1, 0, 3, 2)).reshape(G, T * Cg, N)
    bm_im = jnp.transpose(bm_im, (1, 0, 3, 2)).reshape(G, T * Cg, N)
    o_re = cr[None] * p_re[1:, :, None, :] - ci[None] * p_im[1:, :, None, :]
    o_im = -cr[None] * p_im[1:, :, None, :] - ci[None] * p_re[1:, :, None, :]
    cm_re = jnp.transpose(o_re, (1, 3, 0, 2)).reshape(G, N, T * Cg)
    cm_im = jnp.transpose(o_im, (1, 3, 0, 2)).reshape(G, N, T * Cg)
    GP = G // 2
    z_b = jnp.zeros((GP, T * Cg, N), f32)
    bm_r = bm_re.reshape(GP, 2, T * Cg, N)
    bm_i = bm_im.reshape(GP, 2, T * Cg, N)
    bm_pair = jnp.concatenate([
        jnp.concatenate([bm_r[:, 0], z_b, bm_i[:, 0], z_b], axis=2),
        jnp.concatenate([z_b, bm_r[:, 1], z_b, bm_i[:, 1]], axis=2)], axis=1)
    z_c = jnp.zeros((GP, N, T * Cg), f32)
    cm_r = cm_re.reshape(GP, 2, N, T * Cg)
    cm_i = cm_im.reshape(GP, 2, N, T * Cg)
    cm_pair = jnp.concatenate([
        jnp.concatenate([cm_r[:, 0], z_c], axis=2),
        jnp.concatenate([z_c, cm_r[:, 1]], axis=2),
        jnp.concatenate([cm_i[:, 0], z_c], axis=2),
        jnp.concatenate([z_c, cm_i[:, 1]], axis=2)], axis=1)
    lt_re = p_re[T].reshape(GP, 2 * N)
    lt_im = p_im[T].reshape(GP, 2 * N)
    lam_t = jnp.stack([lt_re, lt_im], axis=0).reshape(2, GP * 2 * N)
    return m_mat, bm_pair, cm_pair, lam_t


def _hi_lo(w):
    hi = w.astype(BF16)
    lo = (w - hi.astype(F32)).astype(BF16)
    return hi, lo


def _s5_in_kernel(u_ref, bh_ref, bl_ref, x_ref):
    u = jnp.concatenate([u_ref[0], u_ref[1]], axis=1)
    x_ref[...] = _dot_x3(u, bh_ref[...], bl_ref[...])


def _s5_scan_kernel(x_ref, lam_ref, hs_ref, h_scr, *, tj):
    @pl.when(pl.program_id(0) == 0)
    def _():
        h_scr[...] = jnp.zeros_like(h_scr)

    nb = h_scr.shape[0]
    npair = SSM_GROUPS // 2
    lre = jnp.broadcast_to(lam_ref[0:1, :], (nb, npair * 128))
    lim = jnp.broadcast_to(lam_ref[1:2, :], (nb, npair * 128))

    def body(j, h):
        hs_ref[j] = h
        x = x_ref[j]
        parts = []
        for p in range(npair):
            hre = h[:, 256 * p:256 * p + 128]
            him = h[:, 256 * p + 128:256 * p + 256]
            pr = lre[:, 128 * p:128 * p + 128]
            pi = lim[:, 128 * p:128 * p + 128]
            parts.append(pr * hre - pi * him + x[:, 256 * p:256 * p + 128])
            parts.append(pr * him + pi * hre + x[:, 256 * p + 128:256 * p + 256])
        return jnp.concatenate(parts, axis=1)

    h_scr[...] = lax.fori_loop(0, tj, body, h_scr[...])


def _s5_out_kernel(u_ref, hs_ref, mh_ref, ml_ref, ch_ref, cl_ref, y_ref):
    yh = _dot_x3(hs_ref[...], ch_ref[...], cl_ref[...])
    w = S5_T * SSM_GROUP
    for i in range(2):
        y_ref[i] = _dot_x3(u_ref[i], mh_ref[i], ml_ref[i]) + yh[:, i * w:(i + 1) * w]


def _s5(u_ssm, mats, bsz, seq, tr, tj):
    m_mat, bm_pair, cm_pair, lam_t = mats
    G, Cg, T = SSM_GROUPS, SSM_GROUP, S5_T
    GP = G // 2
    J = seq // T
    R = J * bsz
    W = T * Cg
    u2 = u_ssm.reshape(bsz, J, T, G, Cg).transpose(3, 1, 0, 2, 4).reshape(G, R, W)
    bh, bl = _hi_lo(bm_pair)
    x = pl.pallas_call(
        _s5_in_kernel,
        out_shape=jax.ShapeDtypeStruct((R, GP * 256), F32),
        grid=(GP, R // tr),
        in_specs=[pl.BlockSpec((2, tr, W), lambda p, i: (p, i, 0)),
                  pl.BlockSpec((None, 2 * W, 256), lambda p, i: (p, 0, 0)),
                  pl.BlockSpec((None, 2 * W, 256), lambda p, i: (p, 0, 0))],
        out_specs=pl.BlockSpec((tr, 256), lambda p, i: (i, p)),
        compiler_params=_cparams(("parallel", "parallel")),
        name="s5_in",
    )(u2, bh, bl)
    x3 = x.reshape(J, bsz, GP * 256)
    hs = pl.pallas_call(
        functools.partial(_s5_scan_kernel, tj=tj),
        out_shape=jax.ShapeDtypeStruct((J, bsz, GP * 256), F32),
        grid=(J // tj,),
        in_specs=[pl.BlockSpec((tj, bsz, GP * 256), lambda i: (i, 0, 0)),
                  pl.BlockSpec((2, GP * 128), lambda i: (0, 0))],
        out_specs=pl.BlockSpec((tj, bsz, GP * 256), lambda i: (i, 0, 0)),
        scratch_shapes=[pltpu.VMEM((bsz, GP * 256), F32)],
        compiler_params=_cparams(("arbitrary",)),
        name="s5_scan",
    )(x3, lam_t)
    mh, ml = _hi_lo(m_mat)
    ch, cl = _hi_lo(cm_pair)
    y2 = pl.pallas_call(
        _s5_out_kernel,
        out_shape=jax.ShapeDtypeStruct((G, R, W), F32),
        grid=(GP, R // tr),
        in_specs=[pl.BlockSpec((2, tr, W), lambda p, i: (p, i, 0)),
                  pl.BlockSpec((tr, 256), lambda p, i: (i, p)),
                  pl.BlockSpec((2, W, W), lambda p, i: (p, 0, 0)),
                  pl.BlockSpec((2, W, W), lambda p, i: (p, 0, 0)),
                  pl.BlockSpec((None, 256, 2 * W), lambda p, i: (p, 0, 0)),
                  pl.BlockSpec((None, 256, 2 * W), lambda p, i: (p, 0, 0))],
        out_specs=pl.BlockSpec((2, tr, W), lambda p, i: (p, i, 0)),
        compiler_params=_cparams(("parallel", "parallel")),
        name="s5_out",
    )(u2, hs.reshape(R, GP * 256), mh, ml, ch, cl)
    return y2.reshape(G, J, bsz, T, Cg).transpose(2, 1, 3, 0, 4).reshape(bsz, seq, D_SSM)


def _outproj_kernel(x_ref, yssm_ref, gssm_ref, ydn_ref, ysg_ref, p_ref, wglu_ref, bglu_ref,
                    wout_ref, pleg_ref, wgate_ref, wple_ref, fng_ref, o_ref, *, final):
    y = _gelu(yssm_ref[...])
    y = y * _sigmoid(_bdot(y, wglu_ref[...]) + bglu_ref[...])
    y = y * _silu(gssm_ref[...])
    ycat = jnp.concatenate([y.astype(BF16), ydn_ref[...].astype(BF16), ysg_ref[...].astype(BF16)], axis=1)
    x1 = x_ref[...] + _mm(ycat, wout_ref[...])
    hn = _rms(x1, pleg_ref[...])
    gate = _sigmoid(_bdot(hn, wgate_ref[...]))
    x2 = x1 + gate * _bdot(p_ref[...], wple_ref[...])
    if final:
        x2 = _rms(x2, fng_ref[...])
    o_ref[...] = x2


def _outproj(x2, yssm, gssm, ydn, ysg, p2, w_glu, b_glu, w_out, ple_g, w_gate, w_ple, fn_g, tm, final):
    n = x2.shape[0]
    row = lambda w: pl.BlockSpec((tm, w), lambda i: (i, 0))
    full = lambda a, b: pl.BlockSpec((a, b), lambda i: (0, 0))
    return pl.pallas_call(
        functools.partial(_outproj_kernel, final=final),
        out_shape=jax.ShapeDtypeStruct((n, D_MODEL), F32),
        grid=(n // tm,),
        in_specs=[row(D_MODEL), row(D_SSM), row(D_SSM), row(D_DN), row(D_SG), row(D_PLE),
                  full(D_SSM, D_SSM), full(1, D_SSM), full(D_MODEL, D_MODEL), full(1, D_MODEL),
                  full(D_MODEL, D_MODEL), full(D_PLE, D_MODEL), full(1, D_MODEL)],
        out_specs=row(D_MODEL),
        compiler_params=_cparams(("parallel",)),
        name="outproj",
    )(x2, yssm, gssm, ydn, ysg, p2, w_glu.astype(BF16), b_glu.reshape(1, D_SSM), w_out.astype(BF16),
      ple_g.reshape(1, D_MODEL), w_gate.astype(BF16), w_ple.astype(BF16), fn_g.reshape(1, D_MODEL))


def _permute_w_in(w):
    qkv_end = 2 * D_SSM + 3 * D_DN
    ab_end = qkv_end + 2 * DN_HEADS
    pad = jnp.zeros((w.shape[0], AB_PAD - 2 * DN_HEADS), w.dtype)
    return jnp.concatenate([w[:, :qkv_end], w[:, qkv_end:ab_end], pad, w[:, ab_end:]], axis=1)


def _forward(x, p, norm_g, w_in, ssm_a_re, ssm_a_im, ssm_b_re, ssm_b_im, ssm_c_re, ssm_c_im,
             ssm_d, ssm_log_step, ssm_w_glu, ssm_b_glu, dn_conv_w, dn_a_log, dn_dt_bias, dn_norm_g,
             sg_ln_g, sg_ln_b, sg_w, sg_b, w_out, ple_norm_g, w_ple_gate, w_ple, final_norm_g,
             *, tm, ts, tr, tj):
    bsz, seq, _ = x.shape
    depth = w_in.shape[0]
    n = bsz * seq
    x2 = x.reshape(n, D_MODEL)
    for i in range(depth):
        w_in_p = _permute_w_in(w_in[i]).astype(BF16)
        u_ssm, g_ssm, qkv, ab, g_dn, z_sg = _inproj(x2, norm_g[i], w_in_p, tm)
        mats = _s5_matrices(ssm_a_re[i], ssm_a_im[i], ssm_b_re[i], ssm_b_im[i], ssm_c_re[i], ssm_c_im[i],
                            ssm_d[i], ssm_log_step[i])
        y_ssm = _s5(u_ssm.reshape(bsz, seq, D_SSM), mats, bsz, seq, tr, tj).reshape(n, D_SSM)
        y_dn = _dnet(qkv.reshape(bsz, seq, W_QKV), ab.reshape(bsz, seq, W_AB), g_dn.reshape(bsz, seq, W_GDN),
                     dn_conv_w[i], dn_a_log[i], dn_dt_bias[i], dn_norm_g[i], ts).reshape(n, D_DN)
        y_sg = _sgate(z_sg, sg_ln_g[i], sg_ln_b[i], sg_w[i], sg_b[i], tm)
        x2 = _outproj(x2, y_ssm, g_ssm, y_dn, y_sg, p[i].reshape(n, D_PLE), ssm_w_glu[i], ssm_b_glu[i],
                      w_out[i], ple_norm_g[i], w_ple_gate[i], w_ple[i], final_norm_g, tm,
                      final=(i == depth - 1))
    return x2.reshape(bsz, seq, D_MODEL)


def kernel(x, p, norm_g, w_in, ssm_a_re, ssm_a_im, ssm_b_re, ssm_b_im, ssm_c_re, ssm_c_im, ssm_d, ssm_log_step, ssm_w_glu, ssm_b_glu, dn_conv_w, dn_a_log, dn_dt_bias, dn_norm_g, sg_ln_g, sg_ln_b, sg_w, sg_b, w_out, ple_norm_g, w_ple_gate, w_ple, final_norm_g):
    bsz, seq, _ = x.shape
    tm = 512
    ts = min(512, seq)
    j_chunks = seq // S5_T
    tr = min(1024, j_chunks * bsz)
    tj = min(32, j_chunks)
    return _forward(x, p, norm_g, w_in, ssm_a_re, ssm_a_im, ssm_b_re, ssm_b_im, ssm_c_re, ssm_c_im,
                    ssm_d, ssm_log_step, ssm_w_glu, ssm_b_glu, dn_conv_w, dn_a_log, dn_dt_bias, dn_norm_g,
                    sg_ln_g, sg_ln_b, sg_w, sg_b, w_out, ple_norm_g, w_ple_gate, w_ple, final_norm_g,
                    tm=tm, ts=ts, tr=tr, tj=tj)
```

```python
import functools

import numpy as np
import jax
import jax.numpy as jnp
from jax import lax
from jax.experimental import pallas as pl
from jax.experimental.pallas import tpu as pltpu

F32 = jnp.float32
BF16 = jnp.bfloat16

D_MODEL = 1024
D_PLE = 256
D_SSM = 256
D_DN = 512
D_SG = 256
SSM_GROUP = 16
SSM_GROUPS = 16
SSM_STATE = 64
DN_HEADS = 4
DN_HEAD_DIM = 128
DN_CONV = 4
DN_CHUNK = 64
SG_HEADS = 4
SG_HEAD_DIM = 64
SG_CHUNK = 128
EPS = 1e-6

S5_T = 16
S5_W = S5_T * SSM_GROUP
S5_PAIRS = SSM_GROUPS // 2
S5_LANES = S5_PAIRS * 256
AB_PAD = 128
W_SSM_G, W_QKV, W_AB, W_GDN, W_SG = D_SSM, 3 * D_DN, AB_PAD, D_DN, 3 * D_SG
LANES = 128
IN_WIDTHS = (LANES, LANES, W_SSM_G, W_QKV, W_AB, W_GDN, W_SG)
D_IN_PAD = sum(IN_WIDTHS)

VMEM_LIMIT = 56 * 1024 * 1024


def _cparams(sem):
    return pltpu.CompilerParams(dimension_semantics=sem, vmem_limit_bytes=VMEM_LIMIT)


def _mm(a, b):
    return jnp.dot(a, b, preferred_element_type=F32)


def _mm_nt(a, b):
    return lax.dot_general(a, b, (((1,), (1,)), ((), ())), preferred_element_type=F32)


def _bf(a):
    return a.astype(BF16)


def _bdot(a, b):
    return _mm(_bf(a), _bf(b))


def _split3(a):
    hi = a.astype(BF16)
    r1 = a - hi.astype(F32)
    mid = r1.astype(BF16)
    lo = (r1 - mid.astype(F32)).astype(BF16)
    return hi, mid, lo


def _dot_x3(a, w_hi, w_lo):
    a_hi = a.astype(BF16)
    a_lo = (a - a_hi.astype(F32)).astype(BF16)
    return _mm(a_hi, w_hi) + _mm(a_lo, w_hi) + _mm(a_hi, w_lo)


def _gelu(x):
    return 0.5 * x * (1.0 + jnp.tanh(np.sqrt(2.0 / np.pi).astype(np.float32) * (x + 0.044715 * (x * x * x))))


def _sigmoid(x):
    return 1.0 / (1.0 + jnp.exp(-x))


def _silu(x):
    return x * _sigmoid(x)


def _softplus(x):
    return jnp.maximum(x, 0.0) + jnp.log1p(jnp.exp(-jnp.abs(x)))


def _rms(x, g):
    return x * lax.rsqrt(jnp.mean(x * x, axis=-1, keepdims=True) + EPS) * g


def _inproj_kernel(x_ref, g_ref, w_ref, *out_refs):
    h = _rms(x_ref[...], g_ref[...]).astype(BF16)
    off = 0
    for o_ref, width in zip(out_refs, IN_WIDTHS):
        o_ref[...] = _mm(h, w_ref[:, off:off + width])
        off += width


def _inproj(x2, norm_g, w_in_p, tm):
    n = x2.shape[0]
    outs = tuple(jax.ShapeDtypeStruct((n, w), F32) for w in IN_WIDTHS)
    return pl.pallas_call(
        _inproj_kernel,
        out_shape=outs,
        grid=(n // tm,),
        in_specs=[pl.BlockSpec((tm, D_MODEL), lambda i: (i, 0)),
                  pl.BlockSpec((1, D_MODEL), lambda i: (0, 0)),
                  pl.BlockSpec((D_MODEL, D_IN_PAD), lambda i: (0, 0))],
        out_specs=tuple(pl.BlockSpec((tm, w), lambda i: (i, 0)) for w in IN_WIDTHS),
        compiler_params=_cparams(("parallel",)),
        name="inproj",
    )(x2, norm_g.reshape(1, D_MODEL), w_in_p)


def _sgate_kernel(z_ref, lng_ref, lnb_ref, w_ref, bias_ref, o_ref, *, tm):
    z = z_ref[...]
    u = _gelu(z[:, :D_SG])
    v = _gelu(z[:, D_SG:2 * D_SG])
    gate = z[:, 2 * D_SG:]
    mu = jnp.mean(v, axis=-1, keepdims=True)
    vc = v - mu
    vn = vc * lax.rsqrt(jnp.mean(vc * vc, axis=-1, keepdims=True) + EPS) * lng_ref[...] + lnb_ref[...]
    vn = vn.astype(BF16)
    row = lax.broadcasted_iota(jnp.int32, (SG_CHUNK, SG_CHUNK), 0)
    col = lax.broadcasted_iota(jnp.int32, (SG_CHUNK, SG_CHUNK), 1)
    causal = row >= col
    head_of_lane = jnp.right_shift(lax.broadcasted_iota(jnp.int32, (SG_CHUNK, D_SG), 1), 6)
    w_heads = [jnp.where(causal, w_ref[h], 0.0).astype(BF16) for h in range(SG_HEADS)]
    bias = bias_ref[...]
    for c in range(tm // SG_CHUNK):
        rows = slice(c * SG_CHUNK, (c + 1) * SG_CHUNK)
        vch = vn[rows]
        s = bias
        for h in range(SG_HEADS):
            r = _mm(w_heads[h], vch)
            s = s + jnp.where(head_of_lane == h, r, 0.0)
        o_ref[rows, :] = u[rows] * s * _silu(gate[rows])


def _sgate(z_sg, ln_g, ln_b, w_sp, b_sp, tm):
    n = z_sg.shape[0]
    bias = jnp.repeat(jnp.transpose(b_sp), SG_HEAD_DIM, axis=1)
    return pl.pallas_call(
        functools.partial(_sgate_kernel, tm=tm),
        out_shape=jax.ShapeDtypeStruct((n, D_SG), F32),
        grid=(n // tm,),
        in_specs=[pl.BlockSpec((tm, W_SG), lambda i: (i, 0)),
                  pl.BlockSpec((1, D_SG), lambda i: (0, 0)),
                  pl.BlockSpec((1, D_SG), lambda i: (0, 0)),
                  pl.BlockSpec((SG_HEADS, SG_CHUNK, SG_CHUNK), lambda i: (0, 0, 0)),
                  pl.BlockSpec((SG_CHUNK, D_SG), lambda i: (0, 0))],
        out_specs=pl.BlockSpec((tm, D_SG), lambda i: (i, 0)),
        compiler_params=_cparams(("parallel",)),
        name="sgate",
    )(z_sg, ln_g.reshape(1, D_SG), ln_b.reshape(1, D_SG), w_sp, bias)


DN_PAIR = 2 * DN_CHUNK
DN_BLK = 16


def _dnet_kernel(qkv_ref, ab_ref, gdn_ref, convw_ref, hpar_ref, normg_ref, o_ref,
                 state_ref, halo_ref, *, ts):
    H, Dh, C, P = DN_HEADS, DN_HEAD_DIM, DN_CHUNK, DN_PAIR
    npair = ts // P
    items = [(pp, h) for pp in range(npair) for h in range(H)]

    @pl.when(pl.program_id(1) == 0)
    def _():
        state_ref[...] = jnp.zeros_like(state_ref)
        halo_ref[...] = jnp.zeros_like(halo_ref)

    x = qkv_ref[...]
    w = convw_ref[...]
    xe = jnp.concatenate([halo_ref[...], x], axis=0)
    conv = (x * w[3:4] + xe[7:7 + ts] * w[2:3] + xe[6:6 + ts] * w[1:2] + xe[5:5 + ts] * w[0:1])
    halo_ref[...] = x[ts - 8:ts]
    act = _silu(conv)

    ab = ab_ref[...]
    g_all = -jnp.exp(hpar_ref[0:1, :]) * _softplus(ab + hpar_ref[1:2, :])
    beta_all = _sigmoid(ab)

    row = lax.broadcasted_iota(jnp.int32, (P, P), 0)
    col = lax.broadcasted_iota(jnp.int32, (P, P), 1)
    same_chunk = jnp.right_shift(row, 6) == jnp.right_shift(col, 6)
    causal = jnp.logical_and(same_chunk, row >= col)
    strict = jnp.logical_and(same_chunk, row > col)
    blk = jnp.right_shift(row, 4) == jnp.right_shift(col, 4)
    eye = jnp.where(row == col, 1.0, 0.0).astype(F32)
    tri_b = jnp.where(causal, 1.0, 0.0).astype(BF16)
    first_col = col < C
    first_row1 = lax.broadcasted_iota(jnp.int32, (P, 1), 0) < C

    gc, gct = [], []
    for pp in range(npair):
        g_hi, g_mid, g_lo = _split3(g_all[pp * P:(pp + 1) * P])
        gcp = _mm(tri_b, g_hi) + _mm(tri_b, g_mid) + _mm(tri_b, g_lo)
        gc.append(gcp)
        gct.append(gcp.T)

    def rows(pp):
        return slice(pp * P, (pp + 1) * P)

    qn, kn, kb_b, rhs_b, decay, eg, gcol = [], [], [], [], [], [], []
    for pp, h in items:
        q = act[rows(pp), h * Dh:(h + 1) * Dh]
        k = act[rows(pp), D_DN + h * Dh:D_DN + (h + 1) * Dh]
        v = act[rows(pp), 2 * D_DN + h * Dh:2 * D_DN + (h + 1) * Dh]
        q = q * (lax.rsqrt(jnp.sum(q * q, axis=-1, keepdims=True) + EPS) * (Dh ** -0.5))
        k = k * lax.rsqrt(jnp.sum(k * k, axis=-1, keepdims=True) + EPS)
        gcl = gc[pp][:, h:h + 1]
        grow = gct[pp][h:h + 1, :]
        bcol = beta_all[rows(pp), H + h:H + h + 1]
        diff = gcl - grow
        decay.append(jnp.where(causal, jnp.exp(jnp.where(causal, diff, 0.0)), 0.0))
        e = jnp.exp(gcl)
        kb = k * bcol
        qn.append(q)
        kn.append(k)
        kb_b.append(_bf(kb))
        rhs_b.append(jnp.concatenate([_bf(kb * e), _bf(v * bcol)], axis=1))
        eg.append(e)
        gcol.append(gcl)
    n_items = len(items)
    rng = range(n_items)
    kn_b = [_bf(kn[i]) for i in rng]

    m = [jnp.where(strict, _mm_nt(kb_b[i], kn_b[i]) * decay[i], 0.0) for i in rng]
    d = [jnp.where(blk, m[i], 0.0) for i in rng]
    l_b = [_bf(jnp.where(blk, 0.0, m[i])) for i in rng]
    d_b = [_bf(d[i]) for i in rng]
    d2 = [_mm(d_b[i], d_b[i]) for i in rng]
    d2_b = [_bf(d2[i]) for i in rng]
    d4 = [_mm(d2_b[i], d2_b[i]) for i in rng]
    p1 = [_mm(_bf(eye - d[i]), _bf(eye + d2[i])) for i in rng]
    d4_b = [_bf(d4[i]) for i in rng]
    d8 = [_mm(d4_b[i], d4_b[i]) for i in rng]
    p2 = [_mm(_bf(p1[i]), _bf(eye + d4[i])) for i in rng]
    inv_d_b = [_bf(_mm(_bf(p2[i]), _bf(eye + d8[i]))) for i in rng]
    n1 = [_mm(inv_d_b[i], l_b[i]) for i in rng]
    n1_b = [_bf(n1[i]) for i in rng]
    n2 = [_mm(n1_b[i], n1_b[i]) for i in rng]
    inv_n = [_mm(_bf(eye - n1[i]), _bf(eye + n2[i])) for i in rng]
    tinv = [_mm(_bf(inv_n[i]), inv_d_b[i]) for i in rng]
    sol_b = [_bf(_mm(_bf(tinv[i]), rhs_b[i])) for i in rng]
    attn = [_mm_nt(_bf(qn[i]), kn_b[i]) * decay[i] for i in rng]
    av = [_mm(_bf(attn[i]), sol_b[i]) for i in rng]
    qe, oloc, gb, last = [], [], [], []
    for i, (pp, h) in enumerate(items):
        qe.append(qn[i] * eg[i] - av[i][:, :Dh])
        oloc.append(av[i][:, Dh:])
        g_end0 = gc[pp][C - 1:C, h:h + 1]
        g_end1 = gc[pp][P - 1:P, h:h + 1]
        glast = jnp.where(first_row1, g_end0, g_end1)
        kdt = (kn[i] * jnp.exp(glast - gcol[i])).T
        gb.append((_mm(_bf(jnp.where(first_col, kdt, 0.0)), sol_b[i]),
                   _mm(_bf(jnp.where(first_col, 0.0, kdt)), sol_b[i])))
        last.append((jnp.exp(g_end0), jnp.exp(g_end1)))

    states = [state_ref[h] for h in range(H)]
    normg = normg_ref[...]
    for pp in range(npair):
        for half in range(2):
            r0 = pp * P + half * C
            for h in range(H):
                i = pp * H + h
                g_mat = gb[i][half]
                s_mat = states[h]
                lhs = jnp.concatenate([qe[i][half * C:(half + 1) * C], g_mat[:, :Dh]], axis=0)
                r = _bdot(lhs, s_mat)
                o = r[:C] + oloc[i][half * C:(half + 1) * C]
                states[h] = last[i][half] * s_mat - r[C:] + g_mat[:, Dh:]
                gate = gdn_ref[r0:r0 + C, h * Dh:(h + 1) * Dh]
                o_ref[r0:r0 + C, h * Dh:(h + 1) * Dh] = _rms(o, normg) * _silu(gate)

    for h in range(H):
        state_ref[h] = states[h]


def _dnet(qkv, ab, gdn, conv_w, a_log, dt_bias, norm_g, ts):
    bsz, seq, _ = qkv.shape
    hpar = jnp.zeros((8, AB_PAD), F32)
    hpar = hpar.at[0, :DN_HEADS].set(a_log.astype(F32)).at[1, :DN_HEADS].set(dt_bias.astype(F32))
    return pl.pallas_call(
        functools.partial(_dnet_kernel, ts=ts),
        out_shape=jax.ShapeDtypeStruct((bsz, seq, D_DN), F32),
        grid=(bsz, seq // ts),
        in_specs=[pl.BlockSpec((None, ts, W_QKV), lambda b, j: (b, j, 0)),
                  pl.BlockSpec((None, ts, W_AB), lambda b, j: (b, j, 0)),
                  pl.BlockSpec((None, ts, W_GDN), lambda b, j: (b, j, 0)),
                  pl.BlockSpec((DN_CONV, W_QKV), lambda b, j: (0, 0)),
                  pl.BlockSpec((8, AB_PAD), lambda b, j: (0, 0)),
                  pl.BlockSpec((1, DN_HEAD_DIM), lambda b, j: (0, 0))],
        out_specs=pl.BlockSpec((None, ts, D_DN), lambda b, j: (b, j, 0)),
        scratch_shapes=[pltpu.VMEM((DN_HEADS, DN_HEAD_DIM, DN_HEAD_DIM), F32),
                        pltpu.VMEM((8, W_QKV), F32)],
        compiler_params=_cparams(("parallel", "arbitrary")),
        name="dnet",
    )(qkv, ab, gdn, conv_w, hpar, norm_g.reshape(1, DN_HEAD_DIM))


def _s5_matrices(a_re, a_im, b_re, b_im, c_re, c_im, d_skip, log_step):
    f32 = F32
    G, N, Cg, T, W, GP = SSM_GROUPS, SSM_STATE, SSM_GROUP, S5_T, S5_W, S5_PAIRS
    hp = lax.Precision.HIGHEST
    step = jnp.exp(log_step.astype(f32))[:, None]
    ar, ai = a_re.astype(f32), a_im.astype(f32)
    kk = jnp.arange(T + 1, dtype=f32)[:, None, None]
    pmag = jnp.exp(kk * (ar * step))
    p_re = pmag * jnp.cos(kk * (ai * step))
    p_im = pmag * jnp.sin(kk * (ai * step))
    lam_re, lam_im = p_re[1], p_im[1]
    den = ar * ar + ai * ai
    nr, ni = lam_re - 1.0, lam_im
    f_re = (nr * ar + ni * ai) / den
    f_im = (ni * ar - nr * ai) / den
    br, bi = b_re.astype(f32), b_im.astype(f32)
    bb_re = f_re[..., None] * br - f_im[..., None] * bi
    bb_im = f_re[..., None] * bi + f_im[..., None] * br
    cr, ci = c_re.astype(f32), c_im.astype(f32)
    cp_re = cr[None] * p_re[:T, :, None, :] - ci[None] * p_im[:T, :, None, :]
    cp_im = cr[None] * p_im[:T, :, None, :] + ci[None] * p_re[:T, :, None, :]
    kmat = (jnp.einsum('tgcn,gnd->tgcd', cp_re, bb_re, precision=hp)
            - jnp.einsum('tgcn,gnd->tgcd', cp_im, bb_im, precision=hp))
    kmat = kmat.at[0].add(jnp.eye(Cg, dtype=f32)[None] * d_skip.astype(f32)[:, :, None])
    tau = jnp.arange(T)[None, :] - jnp.arange(T)[:, None]
    ktoe = jnp.where((tau >= 0)[:, :, None, None, None], kmat[jnp.clip(tau, 0, T - 1)], 0.0)
    m5 = jnp.transpose(ktoe, (2, 0, 4, 1, 3))
    q_re = p_re[:T][::-1]
    q_im = p_im[:T][::-1]
    bm_re = jnp.transpose(q_re[..., None] * bb_re[None] - q_im[..., None] * bb_im[None], (1, 0, 3, 2))
    bm_im = jnp.transpose(q_re[..., None] * bb_im[None] + q_im[..., None] * bb_re[None], (1, 0, 3, 2))
    cm_re = jnp.transpose(cr[None] * p_re[1:, :, None, :] - ci[None] * p_im[1:, :, None, :], (1, 0, 3, 2))
    cm_im = jnp.transpose(-cr[None] * p_im[1:, :, None, :] - ci[None] * p_re[1:, :, None, :], (1, 0, 3, 2))
    perm = (jnp.arange(T)[None, :] - jnp.arange(G)[:, None]) % T
    gi = jnp.arange(G)[:, None]
    m5 = m5[gi, perm]
    m5 = jnp.moveaxis(m5, 3, 1)[gi, perm]
    m_mat = jnp.transpose(m5, (0, 2, 3, 1, 4)).reshape(G, W, W)
    bm_re = bm_re[gi, perm].reshape(G, W, N)
    bm_im = bm_im[gi, perm].reshape(G, W, N)
    cm_re = jnp.transpose(cm_re[gi, perm], (0, 2, 1, 3)).reshape(G, N, W)
    cm_im = jnp.transpose(cm_im[gi, perm], (0, 2, 1, 3)).reshape(G, N, W)
    z_b = jnp.zeros((GP, W, N), f32)
    bm_r = bm_re.reshape(GP, 2, W, N)
    bm_i = bm_im.reshape(GP, 2, W, N)
    bm_pair = jnp.concatenate([
        jnp.concatenate([bm_r[:, 0], z_b, bm_i[:, 0], z_b], axis=2),
        jnp.concatenate([z_b, bm_r[:, 1], z_b, bm_i[:, 1]], axis=2)], axis=1)
    z_c = jnp.zeros((GP, N, W), f32)
    cm_r = cm_re.reshape(GP, 2, N, W)
    cm_i = cm_im.reshape(GP, 2, N, W)
    cm_pair = jnp.concatenate([
        jnp.concatenate([cm_r[:, 0], z_c], axis=2),
        jnp.concatenate([z_c, cm_r[:, 1]], axis=2),
        jnp.concatenate([cm_i[:, 0], z_c], axis=2),
        jnp.concatenate([z_c, cm_i[:, 1]], axis=2)], axis=1)
    lam_t = jnp.stack([p_re[T].reshape(GP * 2 * N), p_im[T].reshape(GP * 2 * N)], axis=0)
    return m_mat, bm_pair, cm_pair, lam_t


def _hi_lo(w):
    hi = w.astype(BF16)
    lo = (w - hi.astype(F32)).astype(BF16)
    return hi, lo


def _s5_kernel(u0_ref, u1_ref, bh_ref, bl_ref, mh_ref, ml_ref, ch_ref, cl_ref, lam_ref, y0_ref, y1_ref,
               h_scr, x_scr, hs_scr, *, ts):
    T, G, Cg, W, GP = S5_T, SSM_GROUPS, SSM_GROUP, S5_W, S5_PAIRS
    J = ts // T

    @pl.when(pl.program_id(1) == 0)
    def _():
        h_scr[...] = jnp.zeros_like(h_scr)

    lane_blk = jnp.right_shift(lax.broadcasted_iota(jnp.int32, (1, W), 1), 4)
    masks = [lane_blk == c for c in range(T)]

    def pick(src, fixed):
        out = src[(0 - fixed) % T]
        for b in range(1, T):
            out = jnp.where(masks[b], src[(b - fixed) % T], out)
        return out

    a = []
    for t in range(T):
        u_t = jnp.concatenate([u0_ref[pl.ds(t, J, stride=T), :], u1_ref[pl.ds(t, J, stride=T), :]], axis=1)
        a.append(u_t if t == 0 else pltpu.roll(u_t, Cg * t, axis=1))
    ug = [pick(a, k) for k in range(G)]

    for p in range(GP):
        lhs = jnp.concatenate([ug[2 * p], ug[2 * p + 1]], axis=1)
        xl = _dot_x3(lhs, bh_ref[p], bl_ref[p])
        x_scr[:, :, 256 * p:256 * (p + 1)] = xl.reshape(J // 8, 8, 256)

    lre = lam_ref[0:1, :]
    lim = lam_ref[1:2, :]

    def body(jo, h):
        for r in range(8):
            hs_scr[jo, r:r + 1, :] = h
            x = x_scr[jo, r:r + 1, :]
            parts = []
            for p in range(GP):
                hre = h[:, 256 * p:256 * p + 128]
                him = h[:, 256 * p + 128:256 * p + 256]
                pr = lre[:, 128 * p:128 * p + 128]
                pi = lim[:, 128 * p:128 * p + 128]
                parts.append(pr * hre - pi * him + x[:, 256 * p:256 * p + 128])
                parts.append(pr * him + pi * hre + x[:, 256 * p + 128:256 * p + 256])
            h = jnp.concatenate(parts, axis=1)
        return h

    h_scr[...] = lax.fori_loop(0, J // 8, body, h_scr[...])

    hs = hs_scr[...].reshape(J, S5_LANES)
    yg = []
    for p in range(GP):
        yh = _dot_x3(hs[:, 256 * p:256 * (p + 1)], ch_ref[p], cl_ref[p])
        for i in range(2):
            g = 2 * p + i
            yg.append(_dot_x3(ug[g], mh_ref[g], ml_ref[g]) + yh[:, i * W:(i + 1) * W])
    for t in range(T):
        y_t = pick(yg, t)
        if t:
            y_t = pltpu.roll(y_t, W - Cg * t, axis=1)
        y0_ref[pl.ds(t, J, stride=T), :] = y_t[:, :LANES]
        y1_ref[pl.ds(t, J, stride=T), :] = y_t[:, LANES:]


def _s5(u0, u1, mats, ts):
    bsz, seq, _ = u0.shape
    m_mat, bm_pair, cm_pair, lam_t = mats
    G, W, GP = SSM_GROUPS, S5_W, S5_PAIRS
    J = ts // S5_T
    bh, bl = _hi_lo(bm_pair)
    mh, ml = _hi_lo(m_mat)
    ch, cl = _hi_lo(cm_pair)
    const3 = lambda shape: pl.BlockSpec(shape, lambda b, j: (0, 0, 0))
    half = pl.BlockSpec((None, ts, LANES), lambda b, j: (b, j, 0))
    return pl.pallas_call(
        functools.partial(_s5_kernel, ts=ts),
        out_shape=(jax.ShapeDtypeStruct((bsz, seq, LANES), F32),) * 2,
        grid=(bsz, seq // ts),
        in_specs=[half, half,
                  const3((GP, 2 * W, 256)), const3((GP, 2 * W, 256)),
                  const3((G, W, W)), const3((G, W, W)),
                  const3((GP, 256, 2 * W)), const3((GP, 256, 2 * W)),
                  pl.BlockSpec((2, GP * 128), lambda b, j: (0, 0))],
        out_specs=(half, half),
        scratch_shapes=[pltpu.VMEM((1, S5_LANES), F32),
                        pltpu.VMEM((J // 8, 8, S5_LANES), F32),
                        pltpu.VMEM((J // 8, 8, S5_LANES), F32)],
        compiler_params=_cparams(("parallel", "arbitrary")),
        name="s5",
    )(u0, u1, bh, bl, mh, ml, ch, cl, lam_t)


def _outproj_kernel(x_ref, y0_ref, y1_ref, gssm_ref, ydn_ref, ysg_ref, p_ref, wglu_ref, bglu_ref,
                    wout_ref, pleg_ref, wgate_ref, wple_ref, fng_ref, o_ref, *, final):
    y = _gelu(jnp.concatenate([y0_ref[...], y1_ref[...]], axis=1))
    y = y * _sigmoid(_bdot(y, wglu_ref[...]) + bglu_ref[...])
    y = y * _silu(gssm_ref[...])
    ycat = jnp.concatenate([y.astype(BF16), ydn_ref[...].astype(BF16), ysg_ref[...].astype(BF16)], axis=1)
    x1 = x_ref[...] + _mm(ycat, wout_ref[...])
    hn = _rms(x1, pleg_ref[...])
    gate = _sigmoid(_bdot(hn, wgate_ref[...]))
    x2 = x1 + gate * _bdot(p_ref[...], wple_ref[...])
    if final:
        x2 = _rms(x2, fng_ref[...])
    o_ref[...] = x2


def _outproj(x2, y0, y1, gssm, ydn, ysg, p2, w_glu, b_glu, w_out, ple_g, w_gate, w_ple, fn_g, tm, final):
    n = x2.shape[0]
    row = lambda w: pl.BlockSpec((tm, w), lambda i: (i, 0))
    full = lambda a, b: pl.BlockSpec((a, b), lambda i: (0, 0))
    return pl.pallas_call(
        functools.partial(_outproj_kernel, final=final),
        out_shape=jax.ShapeDtypeStruct((n, D_MODEL), F32),
        grid=(n // tm,),
        in_specs=[row(D_MODEL), row(LANES), row(LANES), row(D_SSM), row(D_DN), row(D_SG), row(D_PLE),
                  full(D_SSM, D_SSM), full(1, D_SSM), full(D_MODEL, D_MODEL), full(1, D_MODEL),
                  full(D_MODEL, D_MODEL), full(D_PLE, D_MODEL), full(1, D_MODEL)],
        out_specs=row(D_MODEL),
        compiler_params=_cparams(("parallel",)),
        name="outproj",
    )(x2, y0, y1, gssm, ydn, ysg, p2, w_glu.astype(BF16), b_glu.reshape(1, D_SSM), w_out.astype(BF16),
      ple_g.reshape(1, D_MODEL), w_gate.astype(BF16), w_ple.astype(BF16), fn_g.reshape(1, D_MODEL))


def _permute_w_in(w):
    qkv_end = 2 * D_SSM + 3 * D_DN
    ab_end = qkv_end + 2 * DN_HEADS
    pad = jnp.zeros((w.shape[0], AB_PAD - 2 * DN_HEADS), w.dtype)
    return jnp.concatenate([w[:, :qkv_end], w[:, qkv_end:ab_end], pad, w[:, ab_end:]], axis=1)


def _forward(x, p, norm_g, w_in, ssm_a_re, ssm_a_im, ssm_b_re, ssm_b_im, ssm_c_re, ssm_c_im,
             ssm_d, ssm_log_step, ssm_w_glu, ssm_b_glu, dn_conv_w, dn_a_log, dn_dt_bias, dn_norm_g,
             sg_ln_g, sg_ln_b, sg_w, sg_b, w_out, ple_norm_g, w_ple_gate, w_ple, final_norm_g,
             *, tm, ts_dn, ts_s5):
    bsz, seq, _ = x.shape
    depth = w_in.shape[0]
    n = bsz * seq
    x2 = x.reshape(n, D_MODEL)
    for i in range(depth):
        w_in_p = _permute_w_in(w_in[i]).astype(BF16)
        u0, u1, g_ssm, qkv, ab, g_dn, z_sg = _inproj(x2, norm_g[i], w_in_p, tm)
        mats = _s5_matrices(ssm_a_re[i], ssm_a_im[i], ssm_b_re[i], ssm_b_im[i], ssm_c_re[i], ssm_c_im[i],
                            ssm_d[i], ssm_log_step[i])
        y0, y1 = _s5(u0.reshape(bsz, seq, LANES), u1.reshape(bsz, seq, LANES), mats, ts_s5)
        y_dn = _dnet(qkv.reshape(bsz, seq, W_QKV), ab.reshape(bsz, seq, W_AB), g_dn.reshape(bsz, seq, W_GDN),
                     dn_conv_w[i], dn_a_log[i], dn_dt_bias[i], dn_norm_g[i], ts_dn).reshape(n, D_DN)
        y_sg = _sgate(z_sg, sg_ln_g[i], sg_ln_b[i], sg_w[i], sg_b[i], tm)
        x2 = _outproj(x2, y0.reshape(n, LANES), y1.reshape(n, LANES), g_ssm, y_dn, y_sg, p[i].reshape(n, D_PLE), ssm_w_glu[i], ssm_b_glu[i],
                      w_out[i], ple_norm_g[i], w_ple_gate[i], w_ple[i], final_norm_g, tm,
                      final=(i == depth - 1))
    return x2.reshape(bsz, seq, D_MODEL)


def kernel(x, p, norm_g, w_in, ssm_a_re, ssm_a_im, ssm_b_re, ssm_b_im, ssm_c_re, ssm_c_im, ssm_d, ssm_log_step, ssm_w_glu, ssm_b_glu, dn_conv_w, dn_a_log, dn_dt_bias, dn_norm_g, sg_ln_g, sg_ln_b, sg_w, sg_b, w_out, ple_norm_g, w_ple_gate, w_ple, final_norm_g):
    seq = x.shape[1]
    return _forward(x, p, norm_g, w_in, ssm_a_re, ssm_a_im, ssm_b_re, ssm_b_im, ssm_c_re, ssm_c_im,
                    ssm_d, ssm_log_step, ssm_w_glu, ssm_b_glu, dn_conv_w, dn_a_log, dn_dt_bias, dn_norm_g,
                    sg_ln_g, sg_ln_b, sg_w, sg_b, w_out, ple_norm_g, w_ple_gate, w_ple, final_norm_g,
                    tm=512, ts_dn=min(512, seq), ts_s5=min(2048, seq))
```

```python
import functools

import numpy as np
import jax
import jax.numpy as jnp
from jax import lax
from jax.experimental import pallas as pl
from jax.experimental.pallas import tpu as pltpu

F32 = jnp.float32
BF16 = jnp.bfloat16

D_MODEL = 1024
D_PLE = 256
D_SSM = 256
D_DN = 512
D_SG = 256
SSM_GROUP = 16
SSM_GROUPS = 16
SSM_STATE = 64
DN_HEADS = 4
DN_HEAD_DIM = 128
DN_CONV = 4
DN_CHUNK = 64
SG_HEADS = 4
SG_HEAD_DIM = 64
SG_CHUNK = 128
EPS = 1e-6

LANES = 128
S5_T = 16
S5_W = S5_T * SSM_GROUP
S5_PAIRS = SSM_GROUPS // 2
S5_LANES = S5_PAIRS * 256
AB_PAD = 128
W_QKV = 3 * D_DN
CONV_BLK = 256
OFF_U, OFF_GSSM, OFF_QKV, OFF_GDN, OFF_SG, OFF_AB = 0, 256, 512, 2048, 2560, 3328
D_IN_PAD = OFF_AB + AB_PAD

VMEM_LIMIT = 56 * 1024 * 1024


def _cparams(sem):
    return pltpu.CompilerParams(dimension_semantics=sem, vmem_limit_bytes=VMEM_LIMIT)


def _mm(a, b):
    return jnp.dot(a, b, preferred_element_type=F32)


def _mm_nt(a, b):
    return lax.dot_general(a, b, (((1,), (1,)), ((), ())), preferred_element_type=F32)


def _bf(a):
    return a.astype(BF16)


def _bdot(a, b):
    return _mm(_bf(a), _bf(b))


def _split3(a):
    hi = a.astype(BF16)
    r1 = a - hi.astype(F32)
    mid = r1.astype(BF16)
    lo = (r1 - mid.astype(F32)).astype(BF16)
    return hi, mid, lo


def _dot_x3(a, w_hi, w_lo):
    a_hi = a.astype(BF16)
    a_lo = (a - a_hi.astype(F32)).astype(BF16)
    return _mm(a_hi, w_hi) + _mm(a_lo, w_hi) + _mm(a_hi, w_lo)


def _gelu(x):
    return 0.5 * x * (1.0 + jnp.tanh(np.sqrt(2.0 / np.pi).astype(np.float32) * (x + 0.044715 * (x * x * x))))


def _sigmoid(x):
    return 1.0 / (1.0 + jnp.exp(-x))


def _silu(x):
    return x * _sigmoid(x)


def _softplus(x):
    return jnp.maximum(x, 0.0) + jnp.log1p(jnp.exp(-jnp.abs(x)))


def _rms(x, g):
    return x * lax.rsqrt(jnp.mean(x * x, axis=-1, keepdims=True) + EPS) * g


def _inproj_kernel(x_ref, g_ref, w_ref, convw_ref, lng_ref, lnb_ref, wsp_ref, bias_ref,
                   u0_ref, u1_ref, gssm_ref, qkv_ref, ab_ref, gdn_ref, ysg_ref, halo_ref, *, tm, tiles_per_seq):
    h = _rms(x_ref[...], g_ref[...]).astype(BF16)

    def proj(off, width):
        return _mm(h, w_ref[:, off:off + width])

    @pl.when(pl.program_id(0) % tiles_per_seq == 0)
    def _():
        halo_ref[...] = jnp.zeros_like(halo_ref)

    def conv_block(x, c0):
        cols = slice(c0, c0 + CONV_BLK)
        w = convw_ref[:, cols]
        xe = jnp.concatenate([halo_ref[:, cols], x], axis=0)
        conv = (x * w[3:4] + xe[7:7 + tm] * w[2:3] + xe[6:6 + tm] * w[1:2] + xe[5:5 + tm] * w[0:1])
        halo_ref[:, cols] = x[tm - 8:tm]
        qkv_ref[:, cols] = _silu(conv).astype(BF16)

    nblk = W_QKV // CONV_BLK
    x_next = proj(OFF_QKV, CONV_BLK)
    for kb in range(nblk):
        x_cur = x_next
        if kb + 1 < nblk:
            x_next = proj(OFF_QKV + (kb + 1) * CONV_BLK, CONV_BLK)
        else:
            z = proj(OFF_SG, 3 * D_SG)
        conv_block(x_cur, kb * CONV_BLK)

    u_ssm = proj(OFF_U, D_SSM)
    u0_ref[...] = u_ssm[:, :LANES]
    u1_ref[...] = u_ssm[:, LANES:]
    v = _gelu(z[:, D_SG:2 * D_SG])
    mu = jnp.mean(v, axis=-1, keepdims=True)
    vc = v - mu
    vn = vc * lax.rsqrt(jnp.mean(vc * vc, axis=-1, keepdims=True) + EPS) * lng_ref[...] + lnb_ref[...]
    gssm_ref[...] = proj(OFF_GSSM, D_SSM).astype(BF16)
    u = _gelu(z[:, :D_SG])
    ug = u * _silu(z[:, 2 * D_SG:])
    head_of_lane = jnp.right_shift(lax.broadcasted_iota(jnp.int32, (1, D_SG), 1), 6)
    wsp = wsp_ref[...]
    bias = bias_ref[...]
    for c in range(tm // SG_CHUNK):
        rows = slice(c * SG_CHUNK, (c + 1) * SG_CHUNK)
        if c < 2:
            gdn_ref[:, c * 256:(c + 1) * 256] = proj(OFF_GDN + c * 256, 256).astype(BF16)
        elif c == 2:
            ab_ref[...] = proj(OFF_AB, AB_PAD)
        vst = jnp.concatenate([_bf(jnp.where(head_of_lane == hh, vn[rows], 0.0)) for hh in range(SG_HEADS)], axis=0)
        s = _mm(wsp, vst) + bias
        ysg_ref[rows, :] = (ug[rows] * s).astype(BF16)


def _inproj(x2, layer, norm_g, w_in_p, conv_w, ln_g, ln_b, wsp, bias, tm, tiles_per_seq):
    n = x2.shape[0]
    row = lambda w: pl.BlockSpec((tm, w), lambda i: (i, 0))
    lay2 = lambda a, b: pl.BlockSpec((None, a, b), lambda i: (layer, 0, 0))
    sds = lambda w, dt: jax.ShapeDtypeStruct((n, w), dt)
    return pl.pallas_call(
        functools.partial(_inproj_kernel, tm=tm, tiles_per_seq=tiles_per_seq),
        out_shape=(sds(LANES, F32), sds(LANES, F32), sds(D_SSM, BF16), sds(W_QKV, BF16), sds(AB_PAD, F32),
                   sds(D_DN, BF16), sds(D_SG, BF16)),
        grid=(n // tm,),
        in_specs=[row(D_MODEL), lay2(1, D_MODEL), lay2(D_MODEL, D_IN_PAD), lay2(DN_CONV, W_QKV),
                  lay2(1, D_SG), lay2(1, D_SG), lay2(SG_CHUNK, SG_HEADS * SG_CHUNK), lay2(SG_CHUNK, D_SG)],
        out_specs=(row(LANES), row(LANES), row(D_SSM), row(W_QKV), row(AB_PAD), row(D_DN), row(D_SG)),
        scratch_shapes=[pltpu.VMEM((8, W_QKV), F32)],
        compiler_params=_cparams(("arbitrary",)),
        name="inproj",
    )(x2, norm_g, w_in_p, conv_w, ln_g, ln_b, wsp, bias)


DN_PAIR = 2 * DN_CHUNK
DN_BLK = 16


def _dnet_kernel(qkv_ref, ab_ref, gdn_ref, hpar_ref, normg_ref, o_ref, state_ref, *, ts):
    H, Dh, C, P = DN_HEADS, DN_HEAD_DIM, DN_CHUNK, DN_PAIR
    npair = ts // P
    items = [(pp, h) for pp in range(npair) for h in range(H)]

    @pl.when(pl.program_id(1) == 0)
    def _():
        state_ref[...] = jnp.zeros_like(state_ref)

    ab = ab_ref[...]
    g_all = -jnp.exp(hpar_ref[0:1, :]) * _softplus(ab + hpar_ref[1:2, :])
    beta_all = _sigmoid(ab)

    row = lax.broadcasted_iota(jnp.int32, (P, P), 0)
    col = lax.broadcasted_iota(jnp.int32, (P, P), 1)
    same_chunk = jnp.right_shift(row, 6) == jnp.right_shift(col, 6)
    causal = jnp.logical_and(same_chunk, row >= col)
    strict = jnp.logical_and(same_chunk, row > col)
    blk = jnp.right_shift(row, 4) == jnp.right_shift(col, 4)
    eye = jnp.where(row == col, 1.0, 0.0).astype(F32)
    tri_b = jnp.where(causal, 1.0, 0.0).astype(BF16)
    first_col = col < C
    first_row1 = lax.broadcasted_iota(jnp.int32, (P, 1), 0) < C

    gc, gct = [], []
    for pp in range(npair):
        g_hi, g_mid, g_lo = _split3(g_all[pp * P:(pp + 1) * P])
        gcp = _mm(tri_b, g_hi) + _mm(tri_b, g_mid) + _mm(tri_b, g_lo)
        gc.append(gcp)
        gct.append(gcp.T)

    def rows(pp):
        return slice(pp * P, (pp + 1) * P)

    qn, kn, kb_b, rhs_b, decay, eg, gcol = [], [], [], [], [], [], []
    for pp, h in items:
        q = qkv_ref[rows(pp), h * Dh:(h + 1) * Dh].astype(F32)
        k = qkv_ref[rows(pp), D_DN + h * Dh:D_DN + (h + 1) * Dh].astype(F32)
        v = qkv_ref[rows(pp), 2 * D_DN + h * Dh:2 * D_DN + (h + 1) * Dh].astype(F32)
        q = q * (lax.rsqrt(jnp.sum(q * q, axis=-1, keepdims=True) + EPS) * (Dh ** -0.5))
        k = k * lax.rsqrt(jnp.sum(k * k, axis=-1, keepdims=True) + EPS)
        gcl = gc[pp][:, h:h + 1]
        grow = gct[pp][h:h + 1, :]
        bcol = beta_all[rows(pp), H + h:H + h + 1]
        diff = gcl - grow
        decay.append(jnp.where(causal, jnp.exp(jnp.where(causal, diff, 0.0)), 0.0))
        e = jnp.exp(gcl)
        kb = k * bcol
        qn.append(q)
        kn.append(k)
        kb_b.append(_bf(kb))
        rhs_b.append(jnp.concatenate([_bf(kb * e), _bf(v * bcol)], axis=1))
        eg.append(e)
        gcol.append(gcl)
    rng = range(len(items))
    kn_b = [_bf(kn[i]) for i in rng]

    m = [jnp.where(strict, _mm_nt(kb_b[i], kn_b[i]) * decay[i], 0.0) for i in rng]
    d = [jnp.where(blk, m[i], 0.0) for i in rng]
    l_b = [_bf(jnp.where(blk, 0.0, m[i])) for i in rng]
    d_b = [_bf(d[i]) for i in rng]
    d2 = [_mm(d_b[i], d_b[i]) for i in rng]
    d2_b = [_bf(d2[i]) for i in rng]
    d4 = [_mm(d2_b[i], d2_b[i]) for i in rng]
    p1 = [_mm(_bf(eye - d[i]), _bf(eye + d2[i])) for i in rng]
    d4_b = [_bf(d4[i]) for i in rng]
    d8 = [_mm(d4_b[i], d4_b[i]) for i in rng]
    p2 = [_mm(_bf(p1[i]), _bf(eye + d4[i])) for i in rng]
    inv_d_b = [_bf(_mm(_bf(p2[i]), _bf(eye + d8[i]))) for i in rng]
    n1 = [_mm(inv_d_b[i], l_b[i]) for i in rng]
    n1_b = [_bf(n1[i]) for i in rng]
    n2 = [_mm(n1_b[i], n1_b[i]) for i in rng]
    inv_n = [_mm(_bf(eye - n1[i]), _bf(eye + n2[i])) for i in rng]
    tinv = [_mm(_bf(inv_n[i]), inv_d_b[i]) for i in rng]
    sol_b = [_bf(_mm(_bf(tinv[i]), rhs_b[i])) for i in rng]
    attn = [_mm_nt(_bf(qn[i]), kn_b[i]) * decay[i] for i in rng]
    av = [_mm(_bf(attn[i]), sol_b[i]) for i in rng]
    qe, oloc, gb, last = [], [], [], []
    for i, (pp, h) in enumerate(items):
        qe.append(qn[i] * eg[i] - av[i][:, :Dh])
        oloc.append(av[i][:, Dh:])
        g_end0 = gc[pp][C - 1:C, h:h + 1]
        g_end1 = gc[pp][P - 1:P, h:h + 1]
        glast = jnp.where(first_row1, g_end0, g_end1)
        kdt = (kn[i] * jnp.exp(glast - gcol[i])).T
        gb.append((_mm(_bf(jnp.where(first_col, kdt, 0.0)), sol_b[i]),
                   _mm(_bf(jnp.where(first_col, 0.0, kdt)), sol_b[i])))
        last.append((jnp.exp(g_end0), jnp.exp(g_end1)))

    states = [state_ref[h] for h in range(H)]
    normg = normg_ref[...]
    for pp in range(npair):
        for half in range(2):
            r0 = pp * P + half * C
            for h in range(H):
                i = pp * H + h
                g_mat = gb[i][half]
                s_mat = states[h]
                lhs = jnp.concatenate([qe[i][half * C:(half + 1) * C], g_mat[:, :Dh]], axis=0)
                r = _bdot(lhs, s_mat)
                o = r[:C] + oloc[i][half * C:(half + 1) * C]
                states[h] = last[i][half] * s_mat - r[C:] + g_mat[:, Dh:]
                gate = gdn_ref[r0:r0 + C, h * Dh:(h + 1) * Dh].astype(F32)
                o_ref[r0:r0 + C, h * Dh:(h + 1) * Dh] = (_rms(o, normg) * _silu(gate)).astype(BF16)

    for h in range(H):
        state_ref[h] = states[h]


def _dnet(qkv, ab, gdn, layer, hpar, norm_g, ts):
    bsz, seq, _ = qkv.shape
    tile = lambda w: pl.BlockSpec((None, ts, w), lambda b, j: (b, j, 0))
    return pl.pallas_call(
        functools.partial(_dnet_kernel, ts=ts),
        out_shape=jax.ShapeDtypeStruct((bsz, seq, D_DN), BF16),
        grid=(bsz, seq // ts),
        in_specs=[tile(W_QKV), tile(AB_PAD), tile(D_DN),
                  pl.BlockSpec((None, 8, AB_PAD), lambda b, j: (layer, 0, 0)),
                  pl.BlockSpec((None, 1, DN_HEAD_DIM), lambda b, j: (layer, 0, 0))],
        out_specs=tile(D_DN),
        scratch_shapes=[pltpu.VMEM((DN_HEADS, DN_HEAD_DIM, DN_HEAD_DIM), F32)],
        compiler_params=_cparams(("parallel", "arbitrary")),
        name="dnet",
    )(qkv, ab, gdn, hpar, norm_g)


_S5_OFF = (np.arange(S5_T)[None, :] - np.arange(SSM_GROUPS)[:, None]) % S5_T
_S5_TAU = _S5_OFF[:, None, :] - _S5_OFF[:, :, None]


def _s5_matrices(a_re, a_im, b_re, b_im, c_re, c_im, d_skip, log_step):
    f32 = F32
    G, N, Cg, T, W, GP = SSM_GROUPS, SSM_STATE, SSM_GROUP, S5_T, S5_W, S5_PAIRS
    hp = lax.Precision.HIGHEST
    step = jnp.exp(log_step.astype(f32))[:, None]
    ar, ai = a_re.astype(f32), a_im.astype(f32)
    wr, wi = ar * step, ai * step

    def lam_pow(e):
        e = jnp.asarray(e, f32)[..., None]
        shp = (G,) + (1,) * (e.ndim - 2) + (N,)
        mag = jnp.exp(e * wr.reshape(shp))
        return mag * jnp.cos(e * wi.reshape(shp)), mag * jnp.sin(e * wi.reshape(shp))

    lam_re, lam_im = lam_pow(np.ones((G,), np.float32))
    den = ar * ar + ai * ai
    nr, ni = lam_re - 1.0, lam_im
    f_re = (nr * ar + ni * ai) / den
    f_im = (ni * ar - nr * ai) / den
    br, bi = b_re.astype(f32), b_im.astype(f32)
    bb_re = f_re[..., None] * br - f_im[..., None] * bi
    bb_im = f_re[..., None] * bi + f_im[..., None] * br
    cr, ci = c_re.astype(f32), c_im.astype(f32)
    valid = (_S5_TAU >= 0)
    pr, pi = lam_pow(np.where(valid, _S5_TAU, 0))
    pr = jnp.where(valid[..., None], pr, 0.0).reshape(G, T * T, N)
    pi = jnp.where(valid[..., None], pi, 0.0).reshape(G, T * T, N)
    crt = jnp.transpose(cr, (0, 2, 1))
    cit = jnp.transpose(ci, (0, 2, 1))
    a_mat = (crt[:, :, :, None] * bb_re[:, :, None, :] - cit[:, :, :, None] * bb_im[:, :, None, :]).reshape(G, N, Cg * Cg)
    b_mat = (cit[:, :, :, None] * bb_re[:, :, None, :] + crt[:, :, :, None] * bb_im[:, :, None, :]).reshape(G, N, Cg * Cg)
    m4 = (jnp.einsum('gxn,gny->gxy', pr, a_mat, precision=hp)
          - jnp.einsum('gxn,gny->gxy', pi, b_mat, precision=hp))
    m5 = jnp.transpose(m4.reshape(G, T, T, Cg, Cg), (0, 1, 4, 2, 3))
    skip = (jnp.eye(T, dtype=f32)[None, :, None, :, None] * jnp.eye(Cg, dtype=f32)[None, None, :, None, :]
            * d_skip.astype(f32)[:, None, None, None, :])
    m_mat = (m5 + skip).reshape(G, W, W)
    qr, qi = lam_pow(T - 1 - _S5_OFF)
    bbt_re = jnp.transpose(bb_re, (0, 2, 1))
    bbt_im = jnp.transpose(bb_im, (0, 2, 1))
    bm_re = (qr[:, :, None, :] * bbt_re[:, None] - qi[:, :, None, :] * bbt_im[:, None]).reshape(G, W, N)
    bm_im = (qr[:, :, None, :] * bbt_im[:, None] + qi[:, :, None, :] * bbt_re[:, None]).reshape(G, W, N)
    er, ei = lam_pow(_S5_OFF + 1)
    ert = jnp.transpose(er, (0, 2, 1))[:, :, :, None]
    eit = jnp.transpose(ei, (0, 2, 1))[:, :, :, None]
    cm_re = (crt[:, :, None, :] * ert - cit[:, :, None, :] * eit).reshape(G, N, W)
    cm_im = (-crt[:, :, None, :] * eit - cit[:, :, None, :] * ert).reshape(G, N, W)
    z_b = jnp.zeros((GP, W, N), f32)
    bm_r = bm_re.reshape(GP, 2, W, N)
    bm_i = bm_im.reshape(GP, 2, W, N)
    bm_pair = jnp.concatenate([
        jnp.concatenate([bm_r[:, 0], z_b, bm_i[:, 0], z_b], axis=2),
        jnp.concatenate([z_b, bm_r[:, 1], z_b, bm_i[:, 1]], axis=2)], axis=1)
    z_c = jnp.zeros((GP, N, W), f32)
    cm_r = cm_re.reshape(GP, 2, N, W)
    cm_i = cm_im.reshape(GP, 2, N, W)
    cm_pair = jnp.concatenate([
        jnp.concatenate([cm_r[:, 0], z_c], axis=2),
        jnp.concatenate([z_c, cm_r[:, 1]], axis=2),
        jnp.concatenate([cm_i[:, 0], z_c], axis=2),
        jnp.concatenate([z_c, cm_i[:, 1]], axis=2)], axis=1)
    tr, ti = lam_pow(np.full((G,), T, np.float32))
    lam_t = jnp.stack([tr.reshape(GP * 2 * N), ti.reshape(GP * 2 * N)], axis=0)
    return m_mat, bm_pair, cm_pair, lam_t


def _hi_lo(w):
    hi = w.astype(BF16)
    lo = (w - hi.astype(F32)).astype(BF16)
    return hi, lo


def _s5_kernel(u0_ref, u1_ref, bh_ref, bl_ref, mh_ref, ml_ref, ch_ref, cl_ref, lam_ref, y0_ref, y1_ref,
               h_scr, x_scr, hs_scr, *, ts):
    T, G, Cg, W, GP = S5_T, SSM_GROUPS, SSM_GROUP, S5_W, S5_PAIRS
    J = ts // T

    @pl.when(pl.program_id(1) == 0)
    def _():
        h_scr[...] = jnp.zeros_like(h_scr)

    lane_blk = jnp.right_shift(lax.broadcasted_iota(jnp.int32, (1, W), 1), 4)
    masks = [lane_blk == c for c in range(T)]

    def pick(src, fixed):
        out = src[(0 - fixed) % T]
        for b in range(1, T):
            out = jnp.where(masks[b], src[(b - fixed) % T], out)
        return out

    a = []
    for t in range(T):
        u_t = jnp.concatenate([u0_ref[pl.ds(t, J, stride=T), :], u1_ref[pl.ds(t, J, stride=T), :]], axis=1)
        a.append(u_t if t == 0 else pltpu.roll(u_t, Cg * t, axis=1))
    ug = [pick(a, k) for k in range(G)]

    for p in range(GP):
        lhs = jnp.concatenate([ug[2 * p], ug[2 * p + 1]], axis=1)
        xl = _dot_x3(lhs, bh_ref[p], bl_ref[p])
        x_scr[:, :, 256 * p:256 * (p + 1)] = xl.reshape(J // 8, 8, 256)

    lre = lam_ref[0:1, :]
    lim = lam_ref[1:2, :]

    def body(jo, h):
        for r in range(8):
            hs_scr[jo, r:r + 1, :] = h
            x = x_scr[jo, r:r + 1, :]
            parts = []
            for p in range(GP):
                hre = h[:, 256 * p:256 * p + 128]
                him = h[:, 256 * p + 128:256 * p + 256]
                pr = lre[:, 128 * p:128 * p + 128]
                pi = lim[:, 128 * p:128 * p + 128]
                parts.append(pr * hre - pi * him + x[:, 256 * p:256 * p + 128])
                parts.append(pr * him + pi * hre + x[:, 256 * p + 128:256 * p + 256])
            h = jnp.concatenate(parts, axis=1)
        return h

    h_scr[...] = lax.fori_loop(0, J // 8, body, h_scr[...])

    hs = hs_scr[...].reshape(J, S5_LANES)
    yg = []
    for p in range(GP):
        yh = _dot_x3(hs[:, 256 * p:256 * (p + 1)], ch_ref[p], cl_ref[p])
        for i in range(2):
            g = 2 * p + i
            yg.append(_dot_x3(ug[g], mh_ref[g], ml_ref[g]) + yh[:, i * W:(i + 1) * W])
    for t in range(T):
        y_t = pick(yg, t)
        if t:
            y_t = pltpu.roll(y_t, W - Cg * t, axis=1)
        y0_ref[pl.ds(t, J, stride=T), :] = y_t[:, :LANES]
        y1_ref[pl.ds(t, J, stride=T), :] = y_t[:, LANES:]


def _s5(u0, u1, layer, mats, ts):
    bsz, seq, _ = u0.shape
    bh, bl, mh, ml, ch, cl, lam_t = mats
    G, W, GP = SSM_GROUPS, S5_W, S5_PAIRS
    J = ts // S5_T
    lay3 = lambda shape: pl.BlockSpec((None,) + shape, lambda b, j: (layer, 0, 0, 0))
    half = pl.BlockSpec((None, ts, LANES), lambda b, j: (b, j, 0))
    return pl.pallas_call(
        functools.partial(_s5_kernel, ts=ts),
        out_shape=(jax.ShapeDtypeStruct((bsz, seq, LANES), F32),) * 2,
        grid=(bsz, seq // ts),
        in_specs=[half, half,
                  lay3((GP, 2 * W, 256)), lay3((GP, 2 * W, 256)),
                  lay3((G, W, W)), lay3((G, W, W)),
                  lay3((GP, 256, 2 * W)), lay3((GP, 256, 2 * W)),
                  pl.BlockSpec((None, 2, GP * 128), lambda b, j: (layer, 0, 0))],
        out_specs=(half, half),
        scratch_shapes=[pltpu.VMEM((1, S5_LANES), F32),
                        pltpu.VMEM((J // 8, 8, S5_LANES), F32),
                        pltpu.VMEM((J // 8, 8, S5_LANES), F32)],
        compiler_params=_cparams(("parallel", "arbitrary")),
        name="s5",
    )(u0, u1, bh, bl, mh, ml, ch, cl, lam_t)


def _outproj_kernel(x_ref, y0_ref, y1_ref, gssm_ref, ydn_ref, ysg_ref, p_ref, wglu_ref, bglu_ref,
                    wout_ref, pleg_ref, wgate_ref, wple_ref, fng_ref, o_ref, *, final):
    y = _gelu(jnp.concatenate([y0_ref[...], y1_ref[...]], axis=1))
    y = y * _sigmoid(_bdot(y, wglu_ref[...]) + bglu_ref[...])
    y = y * _silu(gssm_ref[...].astype(F32))
    ycat = jnp.concatenate([y.astype(BF16), ydn_ref[...], ysg_ref[...]], axis=1)
    x1 = x_ref[...] + _mm(ycat, wout_ref[...])
    hn = _rms(x1, pleg_ref[...])
    gate = _sigmoid(_bdot(hn, wgate_ref[...]))
    x2 = x1 + gate * _bdot(p_ref[...], wple_ref[...])
    if final:
        x2 = _rms(x2, fng_ref[...])
    o_ref[...] = x2


def _outproj(x2, y0, y1, gssm, ydn, ysg, p3, layer, w_glu, b_glu, w_out, ple_g, w_gate, w_ple, fn_g, tm, final):
    n = x2.shape[0]
    row = lambda w: pl.BlockSpec((tm, w), lambda i: (i, 0))
    lay2 = lambda a, b: pl.BlockSpec((None, a, b), lambda i: (layer, 0, 0))
    return pl.pallas_call(
        functools.partial(_outproj_kernel, final=final),
        out_shape=jax.ShapeDtypeStruct((n, D_MODEL), F32),
        grid=(n // tm,),
        in_specs=[row(D_MODEL), row(LANES), row(LANES), row(D_SSM), row(D_DN), row(D_SG),
                  pl.BlockSpec((None, tm, D_PLE), lambda i: (layer, i, 0)),
                  lay2(D_SSM, D_SSM), lay2(1, D_SSM), lay2(D_MODEL, D_MODEL), lay2(1, D_MODEL),
                  lay2(D_MODEL, D_MODEL), lay2(D_PLE, D_MODEL),
                  pl.BlockSpec((1, D_MODEL), lambda i: (0, 0))],
        out_specs=row(D_MODEL),
        compiler_params=_cparams(("parallel",)),
        name="outproj",
    )(x2, y0, y1, gssm, ydn, ysg, p3, w_glu, b_glu, w_out, ple_g, w_gate, w_ple, fn_g)


def _permute_w_in(w):
    qkv_end = 2 * D_SSM + 3 * D_DN
    ab_end = qkv_end + 2 * DN_HEADS
    pad = jnp.zeros(w.shape[:-1] + (AB_PAD - 2 * DN_HEADS,), w.dtype)
    return jnp.concatenate([w[..., :qkv_end], w[..., ab_end:], w[..., qkv_end:ab_end], pad], axis=-1)


def _forward(x, p, norm_g, w_in, ssm_a_re, ssm_a_im, ssm_b_re, ssm_b_im, ssm_c_re, ssm_c_im,
             ssm_d, ssm_log_step, ssm_w_glu, ssm_b_glu, dn_conv_w, dn_a_log, dn_dt_bias, dn_norm_g,
             sg_ln_g, sg_ln_b, sg_w, sg_b, w_out, ple_norm_g, w_ple_gate, w_ple, final_norm_g,
             *, tm, ts_dn, ts_s5):
    bsz, seq, _ = x.shape
    depth = w_in.shape[0]
    n = bsz * seq
    x2 = x.reshape(n, D_MODEL)
    p3 = p.reshape(depth, n, D_PLE)

    w_in_p = _permute_w_in(w_in).astype(BF16)
    causal = jnp.tril(jnp.ones((SG_CHUNK, SG_CHUNK), dtype=bool))
    wsp = jnp.transpose(jnp.where(causal, sg_w, 0.0), (0, 2, 1, 3)).reshape(depth, SG_CHUNK, SG_HEADS * SG_CHUNK)
    wsp = wsp.astype(BF16)
    sg_bias = jnp.repeat(jnp.transpose(sg_b, (0, 2, 1)), SG_HEAD_DIM, axis=2)
    hpar = jnp.zeros((depth, 8, AB_PAD), F32)
    hpar = hpar.at[:, 0, :DN_HEADS].set(dn_a_log.astype(F32)).at[:, 1, :DN_HEADS].set(dn_dt_bias.astype(F32))
    m_mat, bm_pair, cm_pair, lam_t = jax.vmap(_s5_matrices)(
        ssm_a_re, ssm_a_im, ssm_b_re, ssm_b_im, ssm_c_re, ssm_c_im, ssm_d, ssm_log_step)
    s5_mats = _hi_lo(bm_pair) + _hi_lo(m_mat) + _hi_lo(cm_pair) + (lam_t,)
    r3 = lambda a: a.reshape(depth, 1, a.shape[-1])
    w_glu_b, w_out_b, w_gate_b, w_ple_b = (ssm_w_glu.astype(BF16), w_out.astype(BF16),
                                          w_ple_gate.astype(BF16), w_ple.astype(BF16))

    for i in range(depth):
        u0, u1, g_ssm, qkv, ab, g_dn, y_sg = _inproj(
            x2, i, r3(norm_g), w_in_p, dn_conv_w, r3(sg_ln_g), r3(sg_ln_b), wsp, sg_bias, tm, seq // tm)
        b3 = lambda a: a.reshape(bsz, seq, a.shape[-1])
        y0, y1 = _s5(b3(u0), b3(u1), i, s5_mats, ts_s5)
        y_dn = _dnet(b3(qkv), b3(ab), b3(g_dn), i, hpar, r3(dn_norm_g), ts_dn)
        x2 = _outproj(x2, y0.reshape(n, LANES), y1.reshape(n, LANES), g_ssm, y_dn.reshape(n, D_DN), y_sg, p3, i,
                      w_glu_b, r3(ssm_b_glu), w_out_b, r3(ple_norm_g), w_gate_b, w_ple_b,
                      final_norm_g.reshape(1, D_MODEL), tm, final=(i == depth - 1))
    return x2.reshape(bsz, seq, D_MODEL)


def kernel(x, p, norm_g, w_in, ssm_a_re, ssm_a_im, ssm_b_re, ssm_b_im, ssm_c_re, ssm_c_im, ssm_d, ssm_log_step, ssm_w_glu, ssm_b_glu, dn_conv_w, dn_a_log, dn_dt_bias, dn_norm_g, sg_ln_g, sg_ln_b, sg_w, sg_b, w_out, ple_norm_g, w_ple_gate, w_ple, final_norm_g):
    seq = x.shape[1]
    return _forward(x, p, norm_g, w_in, ssm_a_re, ssm_a_im, ssm_b_re, ssm_b_im, ssm_c_re, ssm_c_im,
                    ssm_d, ssm_log_step, ssm_w_glu, ssm_b_glu, dn_conv_w, dn_a_log, dn_dt_bias, dn_norm_g,
                    sg_ln_g, sg_ln_b, sg_w, sg_b, w_out, ple_norm_g, w_ple_gate, w_ple, final_norm_g,
                    tm=512, ts_dn=min(512, seq), ts_s5=min(2048, seq))
```

```python
import functools

import numpy as np
import jax
import jax.numpy as jnp
from jax import lax
from jax.experimental import pallas as pl
from jax.experimental.pallas import tpu as pltpu

F32 = jnp.float32
BF16 = jnp.bfloat16

D_MODEL = 1024
D_PLE = 256
D_SSM = 256
D_DN = 512
D_SG = 256
SSM_GROUP = 16
SSM_GROUPS = 16
SSM_STATE = 64
DN_HEADS = 4
DN_HEAD_DIM = 128
DN_CONV = 4
DN_CHUNK = 64
SG_HEADS = 4
SG_HEAD_DIM = 64
SG_CHUNK = 128
EPS = 1e-6

LANES = 128
S5_T = 16
S5_W = S5_T * SSM_GROUP
S5_PAIRS = SSM_GROUPS // 2
S5_LANES = S5_PAIRS * 256
AB_PAD = 128
W_QKV = 3 * D_DN
CONV_BLK = 256
OFF_U, OFF_GSSM, OFF_QKV, OFF_GDN, OFF_SG, OFF_AB = 0, 256, 512, 2048, 2560, 3328
D_IN_PAD = OFF_AB + AB_PAD

VMEM_LIMIT = 56 * 1024 * 1024


def _cparams(sem):
    return pltpu.CompilerParams(dimension_semantics=sem, vmem_limit_bytes=VMEM_LIMIT)


def _mm(a, b):
    return jnp.dot(a, b, preferred_element_type=F32)


def _mm_nt(a, b):
    return lax.dot_general(a, b, (((1,), (1,)), ((), ())), preferred_element_type=F32)


def _bf(a):
    return a.astype(BF16)


def _bdot(a, b):
    return _mm(_bf(a), _bf(b))


def _split3(a):
    hi = a.astype(BF16)
    r1 = a - hi.astype(F32)
    mid = r1.astype(BF16)
    lo = (r1 - mid.astype(F32)).astype(BF16)
    return hi, mid, lo


def _gelu(x):
    return 0.5 * x * (1.0 + jnp.tanh(np.sqrt(2.0 / np.pi).astype(np.float32) * (x + 0.044715 * (x * x * x))))


def _sigmoid(x):
    return 1.0 / (1.0 + jnp.exp(-x))


def _silu(x):
    return x * _sigmoid(x)


def _softplus(x):
    return jnp.maximum(x, 0.0) + jnp.log1p(jnp.exp(-jnp.abs(x)))


def _rms(x, g):
    return x * lax.rsqrt(jnp.mean(x * x, axis=-1, keepdims=True) + EPS) * g


def _inproj_kernel(x_ref, g_ref, w_ref, convw_ref, lng_ref, lnb_ref, wsp_ref, bias_ref,
                   u0_ref, u1_ref, gssm_ref, qkv_ref, ab_ref, gdn_ref, ysg_ref, halo_ref, *, tm, tiles_per_seq):
    h = _rms(x_ref[...], g_ref[...]).astype(BF16)

    def proj(off, width):
        return _mm(h, w_ref[:, off:off + width])

    @pl.when(pl.program_id(0) % tiles_per_seq == 0)
    def _():
        halo_ref[...] = jnp.zeros_like(halo_ref)

    def conv_block(x, c0):
        cols = slice(c0, c0 + CONV_BLK)
        w = convw_ref[:, cols]
        xe = jnp.concatenate([halo_ref[:, cols], x], axis=0)
        conv = (x * w[3:4] + xe[7:7 + tm] * w[2:3] + xe[6:6 + tm] * w[1:2] + xe[5:5 + tm] * w[0:1])
        halo_ref[:, cols] = x[tm - 8:tm]
        qkv_ref[:, cols] = _silu(conv).astype(BF16)

    head_of_lane = jnp.right_shift(lax.broadcasted_iota(jnp.int32, (1, D_SG), 1), 6)
    sg = {}

    def sg_norm():
        v = _gelu(sg["z"][:, D_SG:2 * D_SG])
        mu = jnp.mean(v, axis=-1, keepdims=True)
        vc = v - mu
        sg["vn"] = vc * lax.rsqrt(jnp.mean(vc * vc, axis=-1, keepdims=True) + EPS) * lng_ref[...] + lnb_ref[...]

    def sg_gate():
        sg["ug"] = _gelu(sg["z"][:, :D_SG]) * _silu(sg["z"][:, 2 * D_SG:])

    def sg_chunk(c):
        rows = slice(c * SG_CHUNK, (c + 1) * SG_CHUNK)
        vst = jnp.concatenate([_bf(jnp.where(head_of_lane == hh, sg["vn"][rows], 0.0)) for hh in range(SG_HEADS)],
                              axis=0)
        s = _mm(wsp_ref[...], vst) + bias_ref[...]
        ysg_ref[rows, :] = (sg["ug"][rows] * s).astype(BF16)

    def plain_u():
        u_ssm = proj(OFF_U, D_SSM)
        u0_ref[...] = u_ssm[:, :LANES]
        u1_ref[...] = u_ssm[:, LANES:]

    def plain_gssm():
        gssm_ref[...] = proj(OFF_GSSM, D_SSM).astype(BF16)

    def plain_gdn(c):
        gdn_ref[:, c * 256:(c + 1) * 256] = proj(OFF_GDN + c * 256, 256).astype(BF16)

    def plain_ab():
        ab_ref[...] = proj(OFF_AB, AB_PAD)

    nblk = W_QKV // CONV_BLK
    mxu_extra = [plain_u, plain_gssm, lambda: plain_gdn(0), lambda: plain_gdn(1), plain_ab]
    vpu_extra = [sg_norm, sg_gate, lambda: (sg_chunk(0), sg_chunk(1)), lambda: (sg_chunk(2), sg_chunk(3))]
    assert tm // SG_CHUNK == 4 and nblk >= len(mxu_extra) and nblk >= len(vpu_extra) + 1
    x_next = proj(OFF_QKV, CONV_BLK)
    sg["z"] = proj(OFF_SG, 3 * D_SG)
    for kb in range(nblk):
        x_cur = x_next
        if kb + 1 < nblk:
            x_next = proj(OFF_QKV + (kb + 1) * CONV_BLK, CONV_BLK)
        if kb < len(mxu_extra):
            mxu_extra[kb]()
        conv_block(x_cur, kb * CONV_BLK)
        if kb < len(vpu_extra):
            vpu_extra[kb]()


def _inproj(x2, layer, norm_g, w_in_p, conv_w, ln_g, ln_b, wsp, bias, tm, tiles_per_seq):
    n = x2.shape[0]
    row = lambda w: pl.BlockSpec((tm, w), lambda i: (i, 0))
    lay2 = lambda a, b: pl.BlockSpec((None, a, b), lambda i: (layer, 0, 0))
    sds = lambda w, dt: jax.ShapeDtypeStruct((n, w), dt)
    return pl.pallas_call(
        functools.partial(_inproj_kernel, tm=tm, tiles_per_seq=tiles_per_seq),
        out_shape=(sds(LANES, F32), sds(LANES, F32), sds(D_SSM, BF16), sds(W_QKV, BF16), sds(AB_PAD, F32),
                   sds(D_DN, BF16), sds(D_SG, BF16)),
        grid=(n // tm,),
        in_specs=[row(D_MODEL), lay2(1, D_MODEL), lay2(D_MODEL, D_IN_PAD), lay2(DN_CONV, W_QKV),
                  lay2(1, D_SG), lay2(1, D_SG), lay2(SG_CHUNK, SG_HEADS * SG_CHUNK), lay2(SG_CHUNK, D_SG)],
        out_specs=(row(LANES), row(LANES), row(D_SSM), row(W_QKV), row(AB_PAD), row(D_DN), row(D_SG)),
        scratch_shapes=[pltpu.VMEM((8, W_QKV), F32)],
        compiler_params=_cparams(("arbitrary",)),
        name="inproj",
    )(x2, norm_g, w_in_p, conv_w, ln_g, ln_b, wsp, bias)


DN_PAIR = 2 * DN_CHUNK
DN_BLK = 16


def _dnet_kernel(qkv_ref, ab_ref, gdn_ref, hpar_ref, normg_ref, o_ref, state_ref, *, ts):
    H, Dh, C, P = DN_HEADS, DN_HEAD_DIM, DN_CHUNK, DN_PAIR
    npair = ts // P
    assert npair >= 2 and npair % 2 == 0

    @pl.when(pl.program_id(1) == 0)
    def _():
        state_ref[...] = jnp.zeros_like(state_ref)

    ab = ab_ref[...]
    g_all = -jnp.exp(hpar_ref[0:1, :]) * _softplus(ab + hpar_ref[1:2, :])
    beta_all = _sigmoid(ab)

    row = lax.broadcasted_iota(jnp.int32, (P, P), 0)
    col = lax.broadcasted_iota(jnp.int32, (P, P), 1)
    same_chunk = jnp.right_shift(row, 6) == jnp.right_shift(col, 6)
    causal = jnp.logical_and(same_chunk, row >= col)
    strict = jnp.logical_and(same_chunk, row > col)
    blk = jnp.right_shift(row, 4) == jnp.right_shift(col, 4)
    eye = jnp.where(row == col, 1.0, 0.0).astype(F32)
    tri_b = jnp.where(causal, 1.0, 0.0).astype(BF16)
    first_col = col < C
    first_row1 = lax.broadcasted_iota(jnp.int32, (P, 1), 0) < C

    def rows(pp):
        return slice(pp * P, (pp + 1) * P)

    def prep(pps):
        g = dict(items=[(pp, h) for pp in pps for h in range(H)], gc={},
                 qn=[], kn=[], kb_b=[], rhs_b=[], decay=[], eg=[], gcol=[])
        for pp in pps:
            g_hi, g_mid, g_lo = _split3(g_all[rows(pp)])
            gcp = _mm(tri_b, g_hi) + _mm(tri_b, g_mid) + _mm(tri_b, g_lo)
            g["gc"][pp] = (gcp, gcp.T)
        for pp, h in g["items"]:
            gcp, gct = g["gc"][pp]
            q = qkv_ref[rows(pp), h * Dh:(h + 1) * Dh].astype(F32)
            k = qkv_ref[rows(pp), D_DN + h * Dh:D_DN + (h + 1) * Dh].astype(F32)
            v = qkv_ref[rows(pp), 2 * D_DN + h * Dh:2 * D_DN + (h + 1) * Dh].astype(F32)
            q = q * (lax.rsqrt(jnp.sum(q * q, axis=-1, keepdims=True) + EPS) * (Dh ** -0.5))
            k = k * lax.rsqrt(jnp.sum(k * k, axis=-1, keepdims=True) + EPS)
            gcl = gcp[:, h:h + 1]
            grow = gct[h:h + 1, :]
            bcol = beta_all[rows(pp), H + h:H + h + 1]
            diff = gcl - grow
            g["decay"].append(jnp.where(causal, jnp.exp(jnp.where(causal, diff, 0.0)), 0.0))
            e = jnp.exp(gcl)
            kb = k * bcol
            g["qn"].append(q)
            g["kn"].append(k)
            g["kb_b"].append(_bf(kb))
            g["rhs_b"].append(jnp.concatenate([_bf(kb * e), _bf(v * bcol)], axis=1))
            g["eg"].append(e)
            g["gcol"].append(gcl)
        return g

    def matmul_stages(g):
        n = range(len(g["items"]))

        def each(fn):
            return [fn(i) for i in n]

        def s_m():
            g["kn_b"] = each(lambda i: _bf(g["kn"][i]))
            g["m"] = each(lambda i: jnp.where(strict, _mm_nt(g["kb_b"][i], g["kn_b"][i]) * g["decay"][i], 0.0))

        def s_d2():
            g["d"] = each(lambda i: jnp.where(blk, g["m"][i], 0.0))
            g["l_b"] = each(lambda i: _bf(jnp.where(blk, 0.0, g["m"][i])))
            g["d_b"] = each(lambda i: _bf(g["d"][i]))
            g["d2"] = each(lambda i: _mm(g["d_b"][i], g["d_b"][i]))

        def s_d4():
            g["d2_b"] = each(lambda i: _bf(g["d2"][i]))
            g["d4"] = each(lambda i: _mm(g["d2_b"][i], g["d2_b"][i]))
            g["p1"] = each(lambda i: _mm(_bf(eye - g["d"][i]), _bf(eye + g["d2"][i])))

        def s_d8():
            g["d4_b"] = each(lambda i: _bf(g["d4"][i]))
            g["d8"] = each(lambda i: _mm(g["d4_b"][i], g["d4_b"][i]))
            g["p2"] = each(lambda i: _mm(_bf(g["p1"][i]), _bf(eye + g["d4"][i])))

        def s_invd():
            g["inv_d_b"] = each(lambda i: _bf(_mm(_bf(g["p2"][i]), _bf(eye + g["d8"][i]))))

        def s_n1():
            g["n1"] = each(lambda i: _mm(g["inv_d_b"][i], g["l_b"][i]))

        def s_n2():
            g["n1_b"] = each(lambda i: _bf(g["n1"][i]))
            g["n2"] = each(lambda i: _mm(g["n1_b"][i], g["n1_b"][i]))

        def s_invn():
            g["inv_n"] = each(lambda i: _mm(_bf(eye - g["n1"][i]), _bf(eye + g["n2"][i])))

        def s_tinv():
            g["tinv"] = each(lambda i: _mm(_bf(g["inv_n"][i]), g["inv_d_b"][i]))

        def s_sol():
            g["sol_b"] = each(lambda i: _bf(_mm(_bf(g["tinv"][i]), g["rhs_b"][i])))
            g["attn"] = each(lambda i: _mm_nt(_bf(g["qn"][i]), g["kn_b"][i]) * g["decay"][i])

        def s_av():
            g["av"] = each(lambda i: _mm(_bf(g["attn"][i]), g["sol_b"][i]))

        def s_post():
            g["qe"], g["oloc"], g["gb"], g["last"] = [], [], [], []
            for i, (pp, h) in enumerate(g["items"]):
                gcp = g["gc"][pp][0]
                g["qe"].append(g["qn"][i] * g["eg"][i] - g["av"][i][:, :Dh])
                g["oloc"].append(g["av"][i][:, Dh:])
                g_end0 = gcp[C - 1:C, h:h + 1]
                g_end1 = gcp[P - 1:P, h:h + 1]
                glast = jnp.where(first_row1, g_end0, g_end1)
                kdt = (g["kn"][i] * jnp.exp(glast - g["gcol"][i])).T
                sol_b = g["sol_b"][i]
                g["gb"].append((_mm(_bf(jnp.where(first_col, kdt, 0.0)), sol_b),
                                _mm(_bf(jnp.where(first_col, 0.0, kdt)), sol_b)))
                g["last"].append((jnp.exp(g_end0), jnp.exp(g_end1)))

        return [s_m, s_d2, s_d4, s_d8, s_invd, s_n1, s_n2, s_invn, s_tinv, s_sol, s_av, s_post]

    states = [state_ref[h] for h in range(H)]
    normg = normg_ref[...]

    def seq_steps(g, pps):
        def step(ip, pp, half):
            r0 = pp * P + half * C
            for h in range(H):
                i = ip * H + h
                g_mat = g["gb"][i][half]
                s_mat = states[h]
                lhs = jnp.concatenate([g["qe"][i][half * C:(half + 1) * C], g_mat[:, :Dh]], axis=0)
                r = _bdot(lhs, s_mat)
                o = r[:C] + g["oloc"][i][half * C:(half + 1) * C]
                states[h] = g["last"][i][half] * s_mat - r[C:] + g_mat[:, Dh:]
                gate = gdn_ref[r0:r0 + C, h * Dh:(h + 1) * Dh].astype(F32)
                o_ref[r0:r0 + C, h * Dh:(h + 1) * Dh] = (_rms(o, normg) * _silu(gate)).astype(BF16)
        return [functools.partial(step, ip, pp, half) for ip, pp in enumerate(pps) for half in range(2)]

    pps_a = list(range(npair // 2))
    pps_b = list(range(npair // 2, npair))
    ga = prep(pps_a)
    st_a = matmul_stages(ga)
    st_a[0]()
    gb_ = prep(pps_b)
    for s in st_a[1:]:
        s()
    st_b = matmul_stages(gb_)
    seq_a = seq_steps(ga, pps_a)
    every = max(1, len(st_b) // (len(seq_a) + 1))
    for k, s in enumerate(st_b):
        s()
        if (k + 1) % every == 0 and seq_a:
            seq_a.pop(0)()
    for s in seq_a:
        s()
    for s in seq_steps(gb_, pps_b):
        s()

    for h in range(H):
        state_ref[h] = states[h]


def _dnet(qkv, ab, gdn, layer, hpar, norm_g, ts):
    bsz, seq, _ = qkv.shape
    tile = lambda w: pl.BlockSpec((None, ts, w), lambda b, j: (b, j, 0))
    return pl.pallas_call(
        functools.partial(_dnet_kernel, ts=ts),
        out_shape=jax.ShapeDtypeStruct((bsz, seq, D_DN), BF16),
        grid=(bsz, seq // ts),
        in_specs=[tile(W_QKV), tile(AB_PAD), tile(D_DN),
                  pl.BlockSpec((None, 8, AB_PAD), lambda b, j: (layer, 0, 0)),
                  pl.BlockSpec((None, 1, DN_HEAD_DIM), lambda b, j: (layer, 0, 0))],
        out_specs=tile(D_DN),
        scratch_shapes=[pltpu.VMEM((DN_HEADS, DN_HEAD_DIM, DN_HEAD_DIM), F32)],
        compiler_params=_cparams(("parallel", "arbitrary")),
        name="dnet",
    )(qkv, ab, gdn, hpar, norm_g)


_S5_OFF = (np.arange(S5_T)[None, :] - np.arange(SSM_GROUPS)[:, None]) % S5_T
_S5_TAU = _S5_OFF[:, None, :] - _S5_OFF[:, :, None]


def _s5_matrices(a_re, a_im, b_re, b_im, c_re, c_im, d_skip, log_step):
    f32 = F32
    G, N, Cg, T, W, GP = SSM_GROUPS, SSM_STATE, SSM_GROUP, S5_T, S5_W, S5_PAIRS
    hp = lax.Precision.HIGHEST
    step = jnp.exp(log_step.astype(f32))[:, None]
    ar, ai = a_re.astype(f32), a_im.astype(f32)
    wr, wi = ar * step, ai * step

    def lam_pow(e):
        e = jnp.asarray(e, f32)[..., None]
        shp = (G,) + (1,) * (e.ndim - 2) + (N,)
        mag = jnp.exp(e * wr.reshape(shp))
        return mag * jnp.cos(e * wi.reshape(shp)), mag * jnp.sin(e * wi.reshape(shp))

    lam_re, lam_im = lam_pow(np.ones((G,), np.float32))
    den = ar * ar + ai * ai
    nr, ni = lam_re - 1.0, lam_im
    f_re = (nr * ar + ni * ai) / den
    f_im = (ni * ar - nr * ai) / den
    br, bi = b_re.astype(f32), b_im.astype(f32)
    bb_re = f_re[..., None] * br - f_im[..., None] * bi
    bb_im = f_re[..., None] * bi + f_im[..., None] * br
    cr, ci = c_re.astype(f32), c_im.astype(f32)
    valid = (_S5_TAU >= 0)
    pr, pi = lam_pow(np.where(valid, _S5_TAU, 0))
    pr = jnp.where(valid[..., None], pr, 0.0).reshape(G, T * T, N)
    pi = jnp.where(valid[..., None], pi, 0.0).reshape(G, T * T, N)
    crt = jnp.transpose(cr, (0, 2, 1))
    cit = jnp.transpose(ci, (0, 2, 1))
    a_mat = (crt[:, :, :, None] * bb_re[:, :, None, :] - cit[:, :, :, None] * bb_im[:, :, None, :]).reshape(G, N, Cg * Cg)
    b_mat = (cit[:, :, :, None] * bb_re[:, :, None, :] + crt[:, :, :, None] * bb_im[:, :, None, :]).reshape(G, N, Cg * Cg)
    m4 = (jnp.einsum('gxn,gny->gxy', pr, a_mat, precision=hp)
          - jnp.einsum('gxn,gny->gxy', pi, b_mat, precision=hp))
    m5 = jnp.transpose(m4.reshape(G, T, T, Cg, Cg), (0, 1, 4, 2, 3))
    skip = (jnp.eye(T, dtype=f32)[None, :, None, :, None] * jnp.eye(Cg, dtype=f32)[None, None, :, None, :]
            * d_skip.astype(f32)[:, None, None, None, :])
    m_mat = (m5 + skip).reshape(G, W, W)
    qr, qi = lam_pow(T - 1 - _S5_OFF)
    bbt_re = jnp.transpose(bb_re, (0, 2, 1))
    bbt_im = jnp.transpose(bb_im, (0, 2, 1))
    bm_re = (qr[:, :, None, :] * bbt_re[:, None] - qi[:, :, None, :] * bbt_im[:, None]).reshape(G, W, N)
    bm_im = (qr[:, :, None, :] * bbt_im[:, None] + qi[:, :, None, :] * bbt_re[:, None]).reshape(G, W, N)
    er, ei = lam_pow(_S5_OFF + 1)
    ert = jnp.transpose(er, (0, 2, 1))[:, :, :, None]
    eit = jnp.transpose(ei, (0, 2, 1))[:, :, :, None]
    cm_re = (crt[:, :, None, :] * ert - cit[:, :, None, :] * eit).reshape(G, N, W)
    cm_im = (-crt[:, :, None, :] * eit - cit[:, :, None, :] * ert).reshape(G, N, W)
    z_b = jnp.zeros((GP, W, N), f32)
    bm_r = bm_re.reshape(GP, 2, W, N)
    bm_i = bm_im.reshape(GP, 2, W, N)
    bm_pair = jnp.concatenate([
        jnp.concatenate([bm_r[:, 0], z_b, bm_i[:, 0], z_b], axis=2),
        jnp.concatenate([z_b, bm_r[:, 1], z_b, bm_i[:, 1]], axis=2)], axis=1)
    z_c = jnp.zeros((GP, N, W), f32)
    cm_r = cm_re.reshape(GP, 2, N, W)
    cm_i = cm_im.reshape(GP, 2, N, W)
    cm_pair = jnp.concatenate([
        jnp.concatenate([cm_r[:, 0], z_c], axis=2),
        jnp.concatenate([z_c, cm_r[:, 1]], axis=2),
        jnp.concatenate([cm_i[:, 0], z_c], axis=2),
        jnp.concatenate([z_c, cm_i[:, 1]], axis=2)], axis=1)
    tr, ti = lam_pow(np.full((G,), T, np.float32))
    lam_t = jnp.stack([tr.reshape(GP * 2 * N), ti.reshape(GP * 2 * N)], axis=0)
    return m_mat, bm_pair, cm_pair, lam_t


def _s5_kernel(u0_ref, u1_ref, bm_ref, mm_ref, cm_ref, lam_ref, y0_ref, y1_ref,
               h_scr, x_scr, hs_scr, *, ts):
    T, G, Cg, W, GP = S5_T, SSM_GROUPS, SSM_GROUP, S5_W, S5_PAIRS
    J = ts // T

    @pl.when(pl.program_id(1) == 0)
    def _():
        h_scr[...] = jnp.zeros_like(h_scr)

    lane_blk = jnp.right_shift(lax.broadcasted_iota(jnp.int32, (1, W), 1), 4)
    bit_set = [jnp.bitwise_and(lane_blk, 1 << s) != 0 for s in range(4)]

    def rotate_by_block(src):
        cur = list(src)
        for s in range(4):
            cur = [jnp.where(bit_set[s], cur[(j + (1 << s)) % T], cur[j]) for j in range(T)]
        return cur

    def pick_all(src):
        rot = rotate_by_block(src)
        return [rot[(-f) % T] for f in range(T)]

    a = []
    for t in range(T):
        u_t = jnp.concatenate([u0_ref[pl.ds(t, J, stride=T), :], u1_ref[pl.ds(t, J, stride=T), :]], axis=1)
        a.append(u_t if t == 0 else pltpu.roll(u_t, Cg * t, axis=1))
    ug = pick_all(a)

    for p in range(GP):
        lhs = jnp.concatenate([ug[2 * p], ug[2 * p + 1]], axis=1)
        xl = _mm(_bf(lhs), bm_ref[p])
        x_scr[:, :, 256 * p:256 * (p + 1)] = xl.reshape(J // 8, 8, 256)

    lre = lam_ref[0:1, :]
    lim = lam_ref[1:2, :]

    def body(jo, h):
        for r in range(8):
            hs_scr[jo, r:r + 1, :] = h
            x = x_scr[jo, r:r + 1, :]
            parts = []
            for p in range(GP):
                hre = h[:, 256 * p:256 * p + 128]
                him = h[:, 256 * p + 128:256 * p + 256]
                pr = lre[:, 128 * p:128 * p + 128]
                pi = lim[:, 128 * p:128 * p + 128]
                parts.append(pr * hre - pi * him + x[:, 256 * p:256 * p + 128])
                parts.append(pr * him + pi * hre + x[:, 256 * p + 128:256 * p + 256])
            h = jnp.concatenate(parts, axis=1)
        return h

    h_scr[...] = lax.fori_loop(0, J // 8, body, h_scr[...])

    hs = hs_scr[...].reshape(J, S5_LANES)
    yg = []
    for p in range(GP):
        yh = _mm(_bf(hs[:, 256 * p:256 * (p + 1)]), cm_ref[p])
        for i in range(2):
            g = 2 * p + i
            yg.append(_mm(_bf(ug[g]), mm_ref[g]) + yh[:, i * W:(i + 1) * W])
    yt = pick_all(yg)
    for t in range(T):
        y_t = yt[t]
        if t:
            y_t = pltpu.roll(y_t, W - Cg * t, axis=1)
        y0_ref[pl.ds(t, J, stride=T), :] = y_t[:, :LANES]
        y1_ref[pl.ds(t, J, stride=T), :] = y_t[:, LANES:]


def _s5(u0, u1, layer, mats, ts):
    bsz, seq, _ = u0.shape
    bm, mm, cm, lam_t = mats
    G, W, GP = SSM_GROUPS, S5_W, S5_PAIRS
    J = ts // S5_T
    lay3 = lambda shape: pl.BlockSpec((None,) + shape, lambda b, j: (layer, 0, 0, 0))
    half = pl.BlockSpec((None, ts, LANES), lambda b, j: (b, j, 0))
    return pl.pallas_call(
        functools.partial(_s5_kernel, ts=ts),
        out_shape=(jax.ShapeDtypeStruct((bsz, seq, LANES), F32),) * 2,
        grid=(bsz, seq // ts),
        in_specs=[half, half,
                  lay3((GP, 2 * W, 256)), lay3((G, W, W)), lay3((GP, 256, 2 * W)),
                  pl.BlockSpec((None, 2, GP * 128), lambda b, j: (layer, 0, 0))],
        out_specs=(half, half),
        scratch_shapes=[pltpu.VMEM((1, S5_LANES), F32),
                        pltpu.VMEM((J // 8, 8, S5_LANES), F32),
                        pltpu.VMEM((J // 8, 8, S5_LANES), F32)],
        compiler_params=_cparams(("parallel", "arbitrary")),
        name="s5",
    )(u0, u1, bm, mm, cm, lam_t)


def _outproj_kernel(x_ref, y0_ref, y1_ref, gssm_ref, ydn_ref, ysg_ref, p_ref, wglu_ref, bglu_ref,
                    wout_ref, pleg_ref, wgate_ref, wple_ref, fng_ref, o_ref, *, final):
    y = _gelu(jnp.concatenate([y0_ref[...], y1_ref[...]], axis=1))
    y = y * _sigmoid(_bdot(y, wglu_ref[...]) + bglu_ref[...])
    y = y * _silu(gssm_ref[...].astype(F32))
    ycat = jnp.concatenate([y.astype(BF16), ydn_ref[...], ysg_ref[...]], axis=1)
    x1 = x_ref[...] + _mm(ycat, wout_ref[...])
    hn = _rms(x1, pleg_ref[...])
    gate = _sigmoid(_bdot(hn, wgate_ref[...]))
    x2 = x1 + gate * _bdot(p_ref[...], wple_ref[...])
    if final:
        x2 = _rms(x2, fng_ref[...])
    o_ref[...] = x2


def _outproj(x2, y0, y1, gssm, ydn, ysg, p3, layer, w_glu, b_glu, w_out, ple_g, w_gate, w_ple, fn_g, tm, final):
    n = x2.shape[0]
    row = lambda w: pl.BlockSpec((tm, w), lambda i: (i, 0))
    lay2 = lambda a, b: pl.BlockSpec((None, a, b), lambda i: (layer, 0, 0))
    return pl.pallas_call(
        functools.partial(_outproj_kernel, final=final),
        out_shape=jax.ShapeDtypeStruct((n, D_MODEL), F32),
        grid=(n // tm,),
        in_specs=[row(D_MODEL), row(LANES), row(LANES), row(D_SSM), row(D_DN), row(D_SG),
                  pl.BlockSpec((None, tm, D_PLE), lambda i: (layer, i, 0)),
                  lay2(D_SSM, D_SSM), lay2(1, D_SSM), lay2(D_MODEL, D_MODEL), lay2(1, D_MODEL),
                  lay2(D_MODEL, D_MODEL), lay2(D_PLE, D_MODEL),
                  pl.BlockSpec((1, D_MODEL), lambda i: (0, 0))],
        out_specs=row(D_MODEL),
        compiler_params=_cparams(("parallel",)),
        name="outproj",
    )(x2, y0, y1, gssm, ydn, ysg, p3, w_glu, b_glu, w_out, ple_g, w_gate, w_ple, fn_g)


def _permute_w_in(w):
    qkv_end = 2 * D_SSM + 3 * D_DN
    ab_end = qkv_end + 2 * DN_HEADS
    pad = jnp.zeros(w.shape[:-1] + (AB_PAD - 2 * DN_HEADS,), w.dtype)
    return jnp.concatenate([w[..., :qkv_end], w[..., ab_end:], w[..., qkv_end:ab_end], pad], axis=-1)


def _forward(x, p, norm_g, w_in, ssm_a_re, ssm_a_im, ssm_b_re, ssm_b_im, ssm_c_re, ssm_c_im,
             ssm_d, ssm_log_step, ssm_w_glu, ssm_b_glu, dn_conv_w, dn_a_log, dn_dt_bias, dn_norm_g,
             sg_ln_g, sg_ln_b, sg_w, sg_b, w_out, ple_norm_g, w_ple_gate, w_ple, final_norm_g,
             *, tm, ts_dn, ts_s5):
    bsz, seq, _ = x.shape
    depth = w_in.shape[0]
    n = bsz * seq
    x2 = x.reshape(n, D_MODEL)
    p3 = p.reshape(depth, n, D_PLE)

    w_in_p = _permute_w_in(w_in).astype(BF16)
    causal = jnp.tril(jnp.ones((SG_CHUNK, SG_CHUNK), dtype=bool))
    wsp = jnp.transpose(jnp.where(causal, sg_w, 0.0), (0, 2, 1, 3)).reshape(depth, SG_CHUNK, SG_HEADS * SG_CHUNK)
    wsp = wsp.astype(BF16)
    sg_bias = jnp.repeat(jnp.transpose(sg_b, (0, 2, 1)), SG_HEAD_DIM, axis=2)
    hpar = jnp.zeros((depth, 8, AB_PAD), F32)
    hpar = hpar.at[:, 0, :DN_HEADS].set(dn_a_log.astype(F32)).at[:, 1, :DN_HEADS].set(dn_dt_bias.astype(F32))
    m_mat, bm_pair, cm_pair, lam_t = jax.vmap(_s5_matrices)(
        ssm_a_re, ssm_a_im, ssm_b_re, ssm_b_im, ssm_c_re, ssm_c_im, ssm_d, ssm_log_step)
    s5_mats = (bm_pair.astype(BF16), m_mat.astype(BF16), cm_pair.astype(BF16), lam_t)
    r3 = lambda a: a.reshape(depth, 1, a.shape[-1])
    w_glu_b, w_out_b, w_gate_b, w_ple_b = (ssm_w_glu.astype(BF16), w_out.astype(BF16),
                                          w_ple_gate.astype(BF16), w_ple.astype(BF16))

    for i in range(depth):
        u0, u1, g_ssm, qkv, ab, g_dn, y_sg = _inproj(
            x2, i, r3(norm_g), w_in_p, dn_conv_w, r3(sg_ln_g), r3(sg_ln_b), wsp, sg_bias, tm, seq // tm)
        b3 = lambda a: a.reshape(bsz, seq, a.shape[-1])
        y0, y1 = _s5(b3(u0), b3(u1), i, s5_mats, ts_s5)
        y_dn = _dnet(b3(qkv), b3(ab), b3(g_dn), i, hpar, r3(dn_norm_g), ts_dn)
        x2 = _outproj(x2, y0.reshape(n, LANES), y1.reshape(n, LANES), g_ssm, y_dn.reshape(n, D_DN), y_sg, p3, i,
                      w_glu_b, r3(ssm_b_glu), w_out_b, r3(ple_norm_g), w_gate_b, w_ple_b,
                      final_norm_g.reshape(1, D_MODEL), tm, final=(i == depth - 1))
    return x2.reshape(bsz, seq, D_MODEL)


def kernel(x, p, norm_g, w_in, ssm_a_re, ssm_a_im, ssm_b_re, ssm_b_im, ssm_c_re, ssm_c_im, ssm_d, ssm_log_step, ssm_w_glu, ssm_b_glu, dn_conv_w, dn_a_log, dn_dt_bias, dn_norm_g, sg_ln_g, sg_ln_b, sg_w, sg_b, w_out, ple_norm_g, w_ple_gate, w_ple, final_norm_g):
    seq = x.shape[1]
    return _forward(x, p, norm_g, w_in, ssm_a_re, ssm_a_im, ssm_b_re, ssm_b_im, ssm_c_re, ssm_c_im,
                    ssm_d, ssm_log_step, ssm_w_glu, ssm_b_glu, dn_conv_w, dn_a_log, dn_dt_bias, dn_norm_g,
                    sg_ln_g, sg_ln_b, sg_w, sg_b, w_out, ple_norm_g, w_ple_gate, w_ple, final_norm_g,
                    tm=512, ts_dn=min(512, seq), ts_s5=min(2048, seq))
```

```python
import functools

import numpy as np
import jax
import jax.numpy as jnp
from jax import lax
from jax.experimental import pallas as pl
from jax.experimental.pallas import tpu as pltpu

F32 = jnp.float32
BF16 = jnp.bfloat16

D_MODEL = 1024
D_PLE = 256
D_SSM = 256
D_DN = 512
D_SG = 256
SSM_GROUP = 16
SSM_GROUPS = 16
SSM_STATE = 64
DN_HEADS = 4
DN_HEAD_DIM = 128
DN_CONV = 4
DN_CHUNK = 64
SG_HEADS = 4
SG_HEAD_DIM = 64
SG_CHUNK = 128
EPS = 1e-6

LANES = 128
S5_T = 16
S5_W = S5_T * SSM_GROUP
S5_PAIRS = SSM_GROUPS // 2
S5_LANES = S5_PAIRS * 256
AB_PAD = 128
W_QKV = 3 * D_DN
CONV_BLK = 256
OFF_U, OFF_GSSM, OFF_QKV, OFF_GDN, OFF_SG, OFF_AB = 0, 256, 512, 2048, 2560, 3328
D_IN_PAD = OFF_AB + AB_PAD

VMEM_LIMIT = 56 * 1024 * 1024


def _cparams(sem):
    return pltpu.CompilerParams(dimension_semantics=sem, vmem_limit_bytes=VMEM_LIMIT)


def _mm(a, b):
    return jnp.dot(a, b, preferred_element_type=F32)


def _mm_nt(a, b):
    return lax.dot_general(a, b, (((1,), (1,)), ((), ())), preferred_element_type=F32)


def _bf(a):
    return a.astype(BF16)


def _bdot(a, b):
    return _mm(_bf(a), _bf(b))


def _split3(a):
    hi = a.astype(BF16)
    r1 = a - hi.astype(F32)
    mid = r1.astype(BF16)
    lo = (r1 - mid.astype(F32)).astype(BF16)
    return hi, mid, lo


def _gelu(x):
    return 0.5 * x * (1.0 + jnp.tanh(np.sqrt(2.0 / np.pi).astype(np.float32) * (x + 0.044715 * (x * x * x))))


def _sigmoid(x):
    return 1.0 / (1.0 + jnp.exp(-x))


def _silu(x):
    return x * _sigmoid(x)


def _softplus(x):
    return jnp.maximum(x, 0.0) + jnp.log1p(jnp.exp(-jnp.abs(x)))


def _rms(x, g):
    return x * lax.rsqrt(jnp.mean(x * x, axis=-1, keepdims=True) + EPS) * g


DN_PAIR = 2 * DN_CHUNK
DN_BLK = 16


def _dnet_tasks(qkv_ref, ab_ref, gdn_ref, hpar_ref, normg_ref, o_ref, state_ref, ts):
    H, Dh, C, P = DN_HEADS, DN_HEAD_DIM, DN_CHUNK, DN_PAIR

    ab = ab_ref[...]
    g_all = -jnp.exp(hpar_ref[0:1, :]) * _softplus(ab + hpar_ref[1:2, :])
    beta_all = _sigmoid(ab)

    row = lax.broadcasted_iota(jnp.int32, (P, P), 0)
    col = lax.broadcasted_iota(jnp.int32, (P, P), 1)
    same_chunk = jnp.right_shift(row, 6) == jnp.right_shift(col, 6)
    causal = jnp.logical_and(same_chunk, row >= col)
    strict = jnp.logical_and(same_chunk, row > col)
    blk = jnp.right_shift(row, 4) == jnp.right_shift(col, 4)
    eye = jnp.where(row == col, 1.0, 0.0).astype(F32)
    tri_b = jnp.where(causal, 1.0, 0.0).astype(BF16)
    first_col = col < C
    first_row1 = lax.broadcasted_iota(jnp.int32, (P, 1), 0) < C

    def rows(pp):
        return slice(pp * P, (pp + 1) * P)

    def prep(pps):
        g = dict(items=[(pp, h) for pp in pps for h in range(H)], gc={},
                 qn=[], kn=[], kb_b=[], rhs_b=[], decay=[], eg=[], gcol=[])
        for pp in pps:
            g_hi, g_mid, g_lo = _split3(g_all[rows(pp)])
            gcp = _mm(tri_b, g_hi) + _mm(tri_b, g_mid) + _mm(tri_b, g_lo)
            g["gc"][pp] = (gcp, gcp.T)
        for pp, h in g["items"]:
            gcp, gct = g["gc"][pp]
            q = qkv_ref[rows(pp), h * Dh:(h + 1) * Dh].astype(F32)
            k = qkv_ref[rows(pp), D_DN + h * Dh:D_DN + (h + 1) * Dh].astype(F32)
            v = qkv_ref[rows(pp), 2 * D_DN + h * Dh:2 * D_DN + (h + 1) * Dh].astype(F32)
            q = q * (lax.rsqrt(jnp.sum(q * q, axis=-1, keepdims=True) + EPS) * (Dh ** -0.5))
            k = k * lax.rsqrt(jnp.sum(k * k, axis=-1, keepdims=True) + EPS)
            gcl = gcp[:, h:h + 1]
            grow = gct[h:h + 1, :]
            bcol = beta_all[rows(pp), H + h:H + h + 1]
            diff = gcl - grow
            g["decay"].append(jnp.where(causal, jnp.exp(jnp.where(causal, diff, 0.0)), 0.0))
            e = jnp.exp(gcl)
            kb = k * bcol
            g["qn"].append(q)
            g["kn"].append(k)
            g["kb_b"].append(_bf(kb))
            g["rhs_b"].append(jnp.concatenate([_bf(kb * e), _bf(v * bcol)], axis=1))
            g["eg"].append(e)
            g["gcol"].append(gcl)
        return g

    def matmul_stages(g):
        n = range(len(g["items"]))

        def each(fn):
            return [fn(i) for i in n]

        def s_m():
            g["kn_b"] = each(lambda i: _bf(g["kn"][i]))
            g["m"] = each(lambda i: jnp.where(strict, _mm_nt(g["kb_b"][i], g["kn_b"][i]) * g["decay"][i], 0.0))

        def s_d2():
            g["d"] = each(lambda i: jnp.where(blk, g["m"][i], 0.0))
            g["l_b"] = each(lambda i: _bf(jnp.where(blk, 0.0, g["m"][i])))
            g["d_b"] = each(lambda i: _bf(g["d"][i]))
            g["d2"] = each(lambda i: _mm(g["d_b"][i], g["d_b"][i]))

        def s_d4():
            g["d2_b"] = each(lambda i: _bf(g["d2"][i]))
            g["d4"] = each(lambda i: _mm(g["d2_b"][i], g["d2_b"][i]))
            g["p1"] = each(lambda i: _mm(_bf(eye - g["d"][i]), _bf(eye + g["d2"][i])))

        def s_d8():
            g["d4_b"] = each(lambda i: _bf(g["d4"][i]))
            g["d8"] = each(lambda i: _mm(g["d4_b"][i], g["d4_b"][i]))
            g["p2"] = each(lambda i: _mm(_bf(g["p1"][i]), _bf(eye + g["d4"][i])))

        def s_invd():
            g["inv_d_b"] = each(lambda i: _bf(_mm(_bf(g["p2"][i]), _bf(eye + g["d8"][i]))))

        def s_n1():
            g["n1"] = each(lambda i: _mm(g["inv_d_b"][i], g["l_b"][i]))

        def s_n2():
            g["n1_b"] = each(lambda i: _bf(g["n1"][i]))
            g["n2"] = each(lambda i: _mm(g["n1_b"][i], g["n1_b"][i]))

        def s_invn():
            g["inv_n"] = each(lambda i: _mm(_bf(eye - g["n1"][i]), _bf(eye + g["n2"][i])))

        def s_tinv():
            g["tinv"] = each(lambda i: _mm(_bf(g["inv_n"][i]), g["inv_d_b"][i]))

        def s_sol():
            g["sol_b"] = each(lambda i: _bf(_mm(_bf(g["tinv"][i]), g["rhs_b"][i])))
            g["attn"] = each(lambda i: _mm_nt(_bf(g["qn"][i]), g["kn_b"][i]) * g["decay"][i])

        def s_av():
            g["av"] = each(lambda i: _mm(_bf(g["attn"][i]), g["sol_b"][i]))

        def s_post():
            g["qe"], g["oloc"], g["gb"], g["last"] = [], [], [], []
            for i, (pp, h) in enumerate(g["items"]):
                gcp = g["gc"][pp][0]
                g["qe"].append(g["qn"][i] * g["eg"][i] - g["av"][i][:, :Dh])
                g["oloc"].append(g["av"][i][:, Dh:])
                g_end0 = gcp[C - 1:C, h:h + 1]
                g_end1 = gcp[P - 1:P, h:h + 1]
                glast = jnp.where(first_row1, g_end0, g_end1)
                kdt = (g["kn"][i] * jnp.exp(glast - g["gcol"][i])).T
                sol_b = g["sol_b"][i]
                g["gb"].append((_mm(_bf(jnp.where(first_col, kdt, 0.0)), sol_b),
                                _mm(_bf(jnp.where(first_col, 0.0, kdt)), sol_b)))
                g["last"].append((jnp.exp(g_end0), jnp.exp(g_end1)))

        return [s_m, s_d2, s_d4, s_d8, s_invd, s_n1, s_n2, s_invn, s_tinv, s_sol, s_av, s_post]

    states = [state_ref[h] for h in range(H)]
    normg = normg_ref[...]

    def seq_steps(g, pps):
        def step(ip, pp, half):
            r0 = pp * P + half * C
            for h in range(H):
                i = ip * H + h
                g_mat = g["gb"][i][half]
                s_mat = states[h]
                lhs = jnp.concatenate([g["qe"][i][half * C:(half + 1) * C], g_mat[:, :Dh]], axis=0)
                r = _bdot(lhs, s_mat)
                o = r[:C] + g["oloc"][i][half * C:(half + 1) * C]
                states[h] = g["last"][i][half] * s_mat - r[C:] + g_mat[:, Dh:]
                gate = gdn_ref[r0:r0 + C, h * Dh:(h + 1) * Dh].astype(F32)
                o_ref[r0:r0 + C, h * Dh:(h + 1) * Dh] = (_rms(o, normg) * _silu(gate)).astype(BF16)
        return [functools.partial(step, ip, pp, half) for ip, pp in enumerate(pps) for half in range(2)]

    def save_states():
        for h in range(H):
            state_ref[h] = states[h]

    return prep, matmul_stages, seq_steps, save_states


def _merge_groups(groups):
    out = dict(items=[], gc={}, qn=[], kn=[], kb_b=[], rhs_b=[], decay=[], eg=[], gcol=[])
    for g in groups:
        out["gc"].update(g["gc"])
        for key in ("items", "qn", "kn", "kb_b", "rhs_b", "decay", "eg", "gcol"):
            out[key].extend(g[key])
    return out


def _front_kernel(x_ref, g_ref, w_ref, convw_ref, lng_ref, lnb_ref, wsp_ref, bias_ref, hpar_ref, normg_ref,
                  u0_ref, u1_ref, gssm_ref, ysg_ref, ydn_ref,
                  halo_ref, state_ref, qkv_scr, ab_scr, gdn_scr, *, ts):
    tm = ts

    @pl.when(pl.program_id(1) == 0)
    def _():
        halo_ref[...] = jnp.zeros_like(halo_ref)
        state_ref[...] = jnp.zeros_like(state_ref)

    h = _rms(x_ref[...], g_ref[...]).astype(BF16)

    def proj(off, width):
        return _mm(h, w_ref[:, off:off + width])

    def conv_block(x, c0):
        cols = slice(c0, c0 + CONV_BLK)
        w = convw_ref[:, cols]
        xe = jnp.concatenate([halo_ref[:, cols], x], axis=0)
        conv = (x * w[3:4] + xe[7:7 + tm] * w[2:3] + xe[6:6 + tm] * w[1:2] + xe[5:5 + tm] * w[0:1])
        halo_ref[:, cols] = x[tm - 8:tm]
        qkv_scr[:, cols] = _silu(conv).astype(BF16)

    head_of_lane = jnp.right_shift(lax.broadcasted_iota(jnp.int32, (1, D_SG), 1), 6)
    sg = {}

    def sg_proj():
        sg["z"] = proj(OFF_SG, 3 * D_SG)

    def sg_norm():
        v = _gelu(sg["z"][:, D_SG:2 * D_SG])
        mu = jnp.mean(v, axis=-1, keepdims=True)
        vc = v - mu
        sg["vn"] = vc * lax.rsqrt(jnp.mean(vc * vc, axis=-1, keepdims=True) + EPS) * lng_ref[...] + lnb_ref[...]

    def sg_gate():
        sg["ug"] = _gelu(sg["z"][:, :D_SG]) * _silu(sg["z"][:, 2 * D_SG:])

    def sg_chunk(c):
        rows = slice(c * SG_CHUNK, (c + 1) * SG_CHUNK)
        vst = jnp.concatenate([_bf(jnp.where(head_of_lane == hh, sg["vn"][rows], 0.0)) for hh in range(SG_HEADS)],
                              axis=0)
        s = _mm(wsp_ref[...], vst) + bias_ref[...]
        ysg_ref[rows, :] = (sg["ug"][rows] * s).astype(BF16)

    def plain_u():
        u_ssm = proj(OFF_U, D_SSM)
        u0_ref[...] = u_ssm[:, :LANES]
        u1_ref[...] = u_ssm[:, LANES:]

    def plain_gssm():
        gssm_ref[...] = proj(OFF_GSSM, D_SSM).astype(BF16)

    def plain_gdn(c):
        gdn_scr[:, c * 256:(c + 1) * 256] = proj(OFF_GDN + c * 256, 256).astype(BF16)

    ab_scr[...] = proj(OFF_AB, AB_PAD)
    nblk = W_QKV // CONV_BLK
    x_next = proj(OFF_QKV, CONV_BLK)
    for kb in range(nblk):
        x_cur = x_next
        if kb + 1 < nblk:
            x_next = proj(OFF_QKV + (kb + 1) * CONV_BLK, CONV_BLK)
        else:
            sg_proj()
        conv_block(x_cur, kb * CONV_BLK)

    prep, matmul_stages, seq_steps, save_states = _dnet_tasks(
        qkv_scr, ab_scr, gdn_scr, hpar_ref, normg_ref, ydn_ref, state_ref, ts)
    npair = ts // DN_PAIR
    mxu_tasks = [plain_u, plain_gssm, lambda: plain_gdn(0), lambda: plain_gdn(1)]
    groups = []
    for pp in range(npair):
        if mxu_tasks:
            mxu_tasks.pop(0)()
        groups.append(prep([pp]))
    for task in mxu_tasks:
        task()
    g = _merge_groups(groups)

    vpu_tasks = [sg_norm, sg_gate] + [functools.partial(sg_chunk, c) for c in range(tm // SG_CHUNK)]
    for stage in matmul_stages(g):
        stage()
        if vpu_tasks:
            vpu_tasks.pop(0)()
    for task in vpu_tasks:
        task()

    for step in seq_steps(g, list(range(npair))):
        step()
    save_states()


def _front(x3, layer, norm_g, w_in_p, conv_w, ln_g, ln_b, wsp, bias, hpar, dn_norm_g, ts):
    bsz, seq, _ = x3.shape
    tile = lambda w: pl.BlockSpec((None, ts, w), lambda b, j: (b, j, 0))
    lay2 = lambda a, c: pl.BlockSpec((None, a, c), lambda b, j: (layer, 0, 0))
    sds = lambda w, dt: jax.ShapeDtypeStruct((bsz, seq, w), dt)
    return pl.pallas_call(
        functools.partial(_front_kernel, ts=ts),
        out_shape=(sds(LANES, F32), sds(LANES, F32), sds(D_SSM, BF16), sds(D_SG, BF16), sds(D_DN, BF16)),
        grid=(bsz, seq // ts),
        in_specs=[tile(D_MODEL), lay2(1, D_MODEL), lay2(D_MODEL, D_IN_PAD), lay2(DN_CONV, W_QKV),
                  lay2(1, D_SG), lay2(1, D_SG), lay2(SG_CHUNK, SG_HEADS * SG_CHUNK), lay2(SG_CHUNK, D_SG),
                  lay2(8, AB_PAD), lay2(1, DN_HEAD_DIM)],
        out_specs=(tile(LANES), tile(LANES), tile(D_SSM), tile(D_SG), tile(D_DN)),
        scratch_shapes=[pltpu.VMEM((8, W_QKV), F32),
                        pltpu.VMEM((DN_HEADS, DN_HEAD_DIM, DN_HEAD_DIM), F32),
                        pltpu.VMEM((ts, W_QKV), BF16),
                        pltpu.VMEM((ts, AB_PAD), F32),
                        pltpu.VMEM((ts, D_DN), BF16)],
        compiler_params=_cparams(("parallel", "arbitrary")),
        name="front",
    )(x3, norm_g, w_in_p, conv_w, ln_g, ln_b, wsp, bias, hpar, dn_norm_g)


_S5_OFF = (np.arange(S5_T)[None, :] - np.arange(SSM_GROUPS)[:, None]) % S5_T
_S5_TAU = _S5_OFF[:, None, :] - _S5_OFF[:, :, None]


def _s5_matrices(a_re, a_im, b_re, b_im, c_re, c_im, d_skip, log_step):
    f32 = F32
    G, N, Cg, T, W, GP = SSM_GROUPS, SSM_STATE, SSM_GROUP, S5_T, S5_W, S5_PAIRS
    hp = lax.Precision.HIGHEST
    step = jnp.exp(log_step.astype(f32))[:, None]
    ar, ai = a_re.astype(f32), a_im.astype(f32)
    wr, wi = ar * step, ai * step

    def lam_pow(e):
        e = jnp.asarray(e, f32)[..., None]
        shp = (G,) + (1,) * (e.ndim - 2) + (N,)
        mag = jnp.exp(e * wr.reshape(shp))
        return mag * jnp.cos(e * wi.reshape(shp)), mag * jnp.sin(e * wi.reshape(shp))

    lam_re, lam_im = lam_pow(np.ones((G,), np.float32))
    den = ar * ar + ai * ai
    nr, ni = lam_re - 1.0, lam_im
    f_re = (nr * ar + ni * ai) / den
    f_im = (ni * ar - nr * ai) / den
    br, bi = b_re.astype(f32), b_im.astype(f32)
    bb_re = f_re[..., None] * br - f_im[..., None] * bi
    bb_im = f_re[..., None] * bi + f_im[..., None] * br
    cr, ci = c_re.astype(f32), c_im.astype(f32)
    valid = (_S5_TAU >= 0)
    pr, pi = lam_pow(np.where(valid, _S5_TAU, 0))
    pr = jnp.where(valid[..., None], pr, 0.0).reshape(G, T * T, N)
    pi = jnp.where(valid[..., None], pi, 0.0).reshape(G, T * T, N)
    crt = jnp.transpose(cr, (0, 2, 1))
    cit = jnp.transpose(ci, (0, 2, 1))
    a_mat = (crt[:, :, :, None] * bb_re[:, :, None, :] - cit[:, :, :, None] * bb_im[:, :, None, :]).reshape(G, N, Cg * Cg)
    b_mat = (cit[:, :, :, None] * bb_re[:, :, None, :] + crt[:, :, :, None] * bb_im[:, :, None, :]).reshape(G, N, Cg * Cg)
    m4 = (jnp.einsum('gxn,gny->gxy', pr, a_mat, precision=hp)
          - jnp.einsum('gxn,gny->gxy', pi, b_mat, precision=hp))
    m5 = jnp.transpose(m4.reshape(G, T, T, Cg, Cg), (0, 1, 4, 2, 3))
    skip = (jnp.eye(T, dtype=f32)[None, :, None, :, None] * jnp.eye(Cg, dtype=f32)[None, None, :, None, :]
            * d_skip.astype(f32)[:, None, None, None, :])
    m_mat = (m5 + skip).reshape(G, W, W)
    qr, qi = lam_pow(T - 1 - _S5_OFF)
    bbt_re = jnp.transpose(bb_re, (0, 2, 1))
    bbt_im = jnp.transpose(bb_im, (0, 2, 1))
    bm_re = (qr[:, :, None, :] * bbt_re[:, None] - qi[:, :, None, :] * bbt_im[:, None]).reshape(G, W, N)
    bm_im = (qr[:, :, None, :] * bbt_im[:, None] + qi[:, :, None, :] * bbt_re[:, None]).reshape(G, W, N)
    er, ei = lam_pow(_S5_OFF + 1)
    ert = jnp.transpose(er, (0, 2, 1))[:, :, :, None]
    eit = jnp.transpose(ei, (0, 2, 1))[:, :, :, None]
    cm_re = (crt[:, :, None, :] * ert - cit[:, :, None, :] * eit).reshape(G, N, W)
    cm_im = (-crt[:, :, None, :] * eit - cit[:, :, None, :] * ert).reshape(G, N, W)
    z_b = jnp.zeros((GP, W, N), f32)
    bm_r = bm_re.reshape(GP, 2, W, N)
    bm_i = bm_im.reshape(GP, 2, W, N)
    bm_pair = jnp.concatenate([
        jnp.concatenate([bm_r[:, 0], z_b, bm_i[:, 0], z_b], axis=2),
        jnp.concatenate([z_b, bm_r[:, 1], z_b, bm_i[:, 1]], axis=2)], axis=1)
    z_c = jnp.zeros((GP, N, W), f32)
    cm_r = cm_re.reshape(GP, 2, N, W)
    cm_i = cm_im.reshape(GP, 2, N, W)
    cm_pair = jnp.concatenate([
        jnp.concatenate([cm_r[:, 0], z_c], axis=2),
        jnp.concatenate([z_c, cm_r[:, 1]], axis=2),
        jnp.concatenate([cm_i[:, 0], z_c], axis=2),
        jnp.concatenate([z_c, cm_i[:, 1]], axis=2)], axis=1)
    tr, ti = lam_pow(np.full((G,), T, np.float32))
    lam_t = jnp.stack([tr.reshape(GP * 2 * N), ti.reshape(GP * 2 * N)], axis=0)
    return m_mat, bm_pair, cm_pair, lam_t


def _s5_kernel(u0_ref, u1_ref, bm_ref, mm_ref, cm_ref, lam_ref, y0_ref, y1_ref,
               h_scr, x_scr, hs_scr, *, ts):
    T, G, Cg, W, GP = S5_T, SSM_GROUPS, SSM_GROUP, S5_W, S5_PAIRS
    J = ts // T

    @pl.when(pl.program_id(1) == 0)
    def _():
        h_scr[...] = jnp.zeros_like(h_scr)

    lane_blk = jnp.right_shift(lax.broadcasted_iota(jnp.int32, (1, W), 1), 4)
    bit_set = [jnp.bitwise_and(lane_blk, 1 << s) != 0 for s in range(4)]

    def rotate_by_block(src):
        cur = list(src)
        for s in range(4):
            cur = [jnp.where(bit_set[s], cur[(j + (1 << s)) % T], cur[j]) for j in range(T)]
        return cur

    def pick_all(src):
        rot = rotate_by_block(src)
        return [rot[(-f) % T] for f in range(T)]

    a = []
    for t in range(T):
        u_t = jnp.concatenate([u0_ref[pl.ds(t, J, stride=T), :], u1_ref[pl.ds(t, J, stride=T), :]], axis=1)
        a.append(u_t if t == 0 else pltpu.roll(u_t, Cg * t, axis=1))
    ug = pick_all(a)

    for p in range(GP):
        lhs = jnp.concatenate([ug[2 * p], ug[2 * p + 1]], axis=1)
        xl = _mm(_bf(lhs), bm_ref[p])
        x_scr[:, :, 256 * p:256 * (p + 1)] = xl.reshape(J // 8, 8, 256)

    lre = lam_ref[0:1, :]
    lim = lam_ref[1:2, :]

    def body(jo, h):
        for r in range(8):
            hs_scr[jo, r:r + 1, :] = h
            x = x_scr[jo, r:r + 1, :]
            parts = []
            for p in range(GP):
                hre = h[:, 256 * p:256 * p + 128]
                him = h[:, 256 * p + 128:256 * p + 256]
                pr = lre[:, 128 * p:128 * p + 128]
                pi = lim[:, 128 * p:128 * p + 128]
                parts.append(pr * hre - pi * him + x[:, 256 * p:256 * p + 128])
                parts.append(pr * him + pi * hre + x[:, 256 * p + 128:256 * p + 256])
            h = jnp.concatenate(parts, axis=1)
        return h

    h_scr[...] = lax.fori_loop(0, J // 8, body, h_scr[...])

    hs = hs_scr[...].reshape(J, S5_LANES)
    yg = []
    for p in range(GP):
        yh = _mm(_bf(hs[:, 256 * p:256 * (p + 1)]), cm_ref[p])
        for i in range(2):
            g = 2 * p + i
            yg.append(_mm(_bf(ug[g]), mm_ref[g]) + yh[:, i * W:(i + 1) * W])
    yt = pick_all(yg)
    for t in range(T):
        y_t = yt[t]
        if t:
            y_t = pltpu.roll(y_t, W - Cg * t, axis=1)
        y0_ref[pl.ds(t, J, stride=T), :] = y_t[:, :LANES]
        y1_ref[pl.ds(t, J, stride=T), :] = y_t[:, LANES:]


def _s5(u0, u1, layer, mats, ts):
    bsz, seq, _ = u0.shape
    bm, mm, cm, lam_t = mats
    G, W, GP = SSM_GROUPS, S5_W, S5_PAIRS
    J = ts // S5_T
    lay3 = lambda shape: pl.BlockSpec((None,) + shape, lambda b, j: (layer, 0, 0, 0))
    half = pl.BlockSpec((None, ts, LANES), lambda b, j: (b, j, 0))
    return pl.pallas_call(
        functools.partial(_s5_kernel, ts=ts),
        out_shape=(jax.ShapeDtypeStruct((bsz, seq, LANES), F32),) * 2,
        grid=(bsz, seq // ts),
        in_specs=[half, half,
                  lay3((GP, 2 * W, 256)), lay3((G, W, W)), lay3((GP, 256, 2 * W)),
                  pl.BlockSpec((None, 2, GP * 128), lambda b, j: (layer, 0, 0))],
        out_specs=(half, half),
        scratch_shapes=[pltpu.VMEM((1, S5_LANES), F32),
                        pltpu.VMEM((J // 8, 8, S5_LANES), F32),
                        pltpu.VMEM((J // 8, 8, S5_LANES), F32)],
        compiler_params=_cparams(("parallel", "arbitrary")),
        name="s5",
    )(u0, u1, bm, mm, cm, lam_t)


def _outproj_kernel(x_ref, y0_ref, y1_ref, gssm_ref, ydn_ref, ysg_ref, p_ref, wglu_ref, bglu_ref,
                    wout_ref, pleg_ref, wgate_ref, wple_ref, fng_ref, o_ref, *, final):
    y = _gelu(jnp.concatenate([y0_ref[...], y1_ref[...]], axis=1))
    y = y * _sigmoid(_bdot(y, wglu_ref[...]) + bglu_ref[...])
    y = y * _silu(gssm_ref[...].astype(F32))
    ycat = jnp.concatenate([y.astype(BF16), ydn_ref[...], ysg_ref[...]], axis=1)
    x1 = x_ref[...] + _mm(ycat, wout_ref[...])
    hn = _rms(x1, pleg_ref[...])
    gate = _sigmoid(_bdot(hn, wgate_ref[...]))
    x2 = x1 + gate * _bdot(p_ref[...], wple_ref[...])
    if final:
        x2 = _rms(x2, fng_ref[...])
    o_ref[...] = x2


def _outproj(x2, y0, y1, gssm, ydn, ysg, p3, layer, w_glu, b_glu, w_out, ple_g, w_gate, w_ple, fn_g, tm, final):
    n = x2.shape[0]
    row = lambda w: pl.BlockSpec((tm, w), lambda i: (i, 0))
    lay2 = lambda a, b: pl.BlockSpec((None, a, b), lambda i: (layer, 0, 0))
    return pl.pallas_call(
        functools.partial(_outproj_kernel, final=final),
        out_shape=jax.ShapeDtypeStruct((n, D_MODEL), F32),
        grid=(n // tm,),
        in_specs=[row(D_MODEL), row(LANES), row(LANES), row(D_SSM), row(D_DN), row(D_SG),
                  pl.BlockSpec((None, tm, D_PLE), lambda i: (layer, i, 0)),
                  lay2(D_SSM, D_SSM), lay2(1, D_SSM), lay2(D_MODEL, D_MODEL), lay2(1, D_MODEL),
                  lay2(D_MODEL, D_MODEL), lay2(D_PLE, D_MODEL),
                  pl.BlockSpec((1, D_MODEL), lambda i: (0, 0))],
        out_specs=row(D_MODEL),
        compiler_params=_cparams(("parallel",)),
        name="outproj",
    )(x2, y0, y1, gssm, ydn, ysg, p3, w_glu, b_glu, w_out, ple_g, w_gate, w_ple, fn_g)


def _permute_w_in(w):
    qkv_end = 2 * D_SSM + 3 * D_DN
    ab_end = qkv_end + 2 * DN_HEADS
    pad = jnp.zeros(w.shape[:-1] + (AB_PAD - 2 * DN_HEADS,), w.dtype)
    return jnp.concatenate([w[..., :qkv_end], w[..., ab_end:], w[..., qkv_end:ab_end], pad], axis=-1)


def _forward(x, p, norm_g, w_in, ssm_a_re, ssm_a_im, ssm_b_re, ssm_b_im, ssm_c_re, ssm_c_im,
             ssm_d, ssm_log_step, ssm_w_glu, ssm_b_glu, dn_conv_w, dn_a_log, dn_dt_bias, dn_norm_g,
             sg_ln_g, sg_ln_b, sg_w, sg_b, w_out, ple_norm_g, w_ple_gate, w_ple, final_norm_g,
             *, tm, ts, ts_s5):
    bsz, seq, _ = x.shape
    depth = w_in.shape[0]
    n = bsz * seq
    p3 = p.reshape(depth, n, D_PLE)

    w_in_p = _permute_w_in(w_in).astype(BF16)
    causal = jnp.tril(jnp.ones((SG_CHUNK, SG_CHUNK), dtype=bool))
    wsp = jnp.transpose(jnp.where(causal, sg_w, 0.0), (0, 2, 1, 3)).reshape(depth, SG_CHUNK, SG_HEADS * SG_CHUNK)
    wsp = wsp.astype(BF16)
    sg_bias = jnp.repeat(jnp.transpose(sg_b, (0, 2, 1)), SG_HEAD_DIM, axis=2)
    hpar = jnp.zeros((depth, 8, AB_PAD), F32)
    hpar = hpar.at[:, 0, :DN_HEADS].set(dn_a_log.astype(F32)).at[:, 1, :DN_HEADS].set(dn_dt_bias.astype(F32))
    m_mat, bm_pair, cm_pair, lam_t = jax.vmap(_s5_matrices)(
        ssm_a_re, ssm_a_im, ssm_b_re, ssm_b_im, ssm_c_re, ssm_c_im, ssm_d, ssm_log_step)
    s5_mats = (bm_pair.astype(BF16), m_mat.astype(BF16), cm_pair.astype(BF16), lam_t)
    r3 = lambda a: a.reshape(depth, 1, a.shape[-1])
    w_glu_b, w_out_b, w_gate_b, w_ple_b = (ssm_w_glu.astype(BF16), w_out.astype(BF16),
                                          w_ple_gate.astype(BF16), w_ple.astype(BF16))

    x3 = x
    for i in range(depth):
        u0, u1, g_ssm, y_sg, y_dn = _front(x3, i, r3(norm_g), w_in_p, dn_conv_w, r3(sg_ln_g), r3(sg_ln_b), wsp, sg_bias,
                                           hpar, r3(dn_norm_g), ts)
        y0, y1 = _s5(u0, u1, i, s5_mats, ts_s5)
        flat = lambda a: a.reshape(n, a.shape[-1])
        x2 = _outproj(flat(x3), flat(y0), flat(y1), flat(g_ssm), flat(y_dn), flat(y_sg), p3, i,
                      w_glu_b, r3(ssm_b_glu), w_out_b, r3(ple_norm_g), w_gate_b, w_ple_b,
                      final_norm_g.reshape(1, D_MODEL), tm, final=(i == depth - 1))
        x3 = x2.reshape(bsz, seq, D_MODEL)
    return x3


def kernel(x, p, norm_g, w_in, ssm_a_re, ssm_a_im, ssm_b_re, ssm_b_im, ssm_c_re, ssm_c_im, ssm_d, ssm_log_step, ssm_w_glu, ssm_b_glu, dn_conv_w, dn_a_log, dn_dt_bias, dn_norm_g, sg_ln_g, sg_ln_b, sg_w, sg_b, w_out, ple_norm_g, w_ple_gate, w_ple, final_norm_g):
    seq = x.shape[1]
    return _forward(x, p, norm_g, w_in, ssm_a_re, ssm_a_im, ssm_b_re, ssm_b_im, ssm_c_re, ssm_c_im,
                    ssm_d, ssm_log_step, ssm_w_glu, ssm_b_glu, dn_conv_w, dn_a_log, dn_dt_bias, dn_norm_g,
                    sg_ln_g, sg_ln_b, sg_w, sg_b, w_out, ple_norm_g, w_ple_gate, w_ple, final_norm_g,
                    tm=512, ts=min(512, seq), ts_s5=min(2048, seq))
```

```python
import functools

import numpy as np
import jax
import jax.numpy as jnp
from jax import lax
from jax.experimental import pallas as pl
from jax.experimental.pallas import tpu as pltpu

F32 = jnp.float32
BF16 = jnp.bfloat16

D_MODEL = 1024
D_PLE = 256
D_SSM = 256
D_DN = 512
D_SG = 256
SSM_GROUP = 16
SSM_GROUPS = 16
SSM_STATE = 64
DN_HEADS = 4
DN_HEAD_DIM = 128
DN_CONV = 4
DN_CHUNK = 64
SG_HEADS = 4
SG_HEAD_DIM = 64
SG_CHUNK = 128
EPS = 1e-6

LANES = 128
S5_T = 16
S5_W = S5_T * SSM_GROUP
S5_PAIRS = SSM_GROUPS // 2
S5_LANES = S5_PAIRS * 256
S5_TAB = 16
AB_PAD = 128
W_QKV = 3 * D_DN
CONV_BLK = 256
OFF_U, OFF_GSSM, OFF_QKV, OFF_GDN, OFF_SG, OFF_AB = 0, 256, 512, 2048, 2560, 3328
D_IN_PAD = OFF_AB + AB_PAD

VMEM_LIMIT = 56 * 1024 * 1024


def _cparams(sem):
    return pltpu.CompilerParams(dimension_semantics=sem, vmem_limit_bytes=VMEM_LIMIT)


def _mm(a, b):
    return jnp.dot(a, b, preferred_element_type=F32)


def _mm_nt(a, b):
    return lax.dot_general(a, b, (((1,), (1,)), ((), ())), preferred_element_type=F32)


def _bf(a):
    return a.astype(BF16)


def _bdot(a, b):
    return _mm(_bf(a), _bf(b))


def _split3(a):
    hi = a.astype(BF16)
    r1 = a - hi.astype(F32)
    mid = r1.astype(BF16)
    lo = (r1 - mid.astype(F32)).astype(BF16)
    return hi, mid, lo


def _gelu(x):
    return 0.5 * x * (1.0 + jnp.tanh(np.sqrt(2.0 / np.pi).astype(np.float32) * (x + 0.044715 * (x * x * x))))


def _sigmoid(x):
    return 0.5 + 0.5 * jnp.tanh(0.5 * x)


def _silu(x):
    hx = 0.5 * x
    return hx + hx * jnp.tanh(hx)


def _softplus(x):
    return jnp.maximum(x, 0.0) + jnp.log1p(jnp.exp(-jnp.abs(x)))


def _rms(x, g):
    return x * lax.rsqrt(jnp.mean(x * x, axis=-1, keepdims=True) + EPS) * g


DN_PAIR = 2 * DN_CHUNK
DN_BLK = 16


def _dnet_tasks(qkv_ref, ab_ref, gdn_ref, hpar_ref, normg_ref, o_ref, state_ref, ts):
    H, Dh, C, P = DN_HEADS, DN_HEAD_DIM, DN_CHUNK, DN_PAIR

    ab = ab_ref[...]
    g_all = -jnp.exp(hpar_ref[0:1, :]) * _softplus(ab + hpar_ref[1:2, :])
    beta_all = _sigmoid(ab)

    row = lax.broadcasted_iota(jnp.int32, (P, P), 0)
    col = lax.broadcasted_iota(jnp.int32, (P, P), 1)
    same_chunk = jnp.right_shift(row, 6) == jnp.right_shift(col, 6)
    causal = jnp.logical_and(same_chunk, row >= col)
    strict = jnp.logical_and(same_chunk, row > col)
    blk = jnp.right_shift(row, 4) == jnp.right_shift(col, 4)
    eye = jnp.where(row == col, 1.0, 0.0).astype(F32)
    tri_b = jnp.where(causal, 1.0, 0.0).astype(BF16)
    first_col = col < C
    first_row1 = lax.broadcasted_iota(jnp.int32, (P, 1), 0) < C

    def rows(pp):
        return slice(pp * P, (pp + 1) * P)

    def prep(pps):
        g = dict(items=[(pp, h) for pp in pps for h in range(H)], gc={},
                 qn=[], kn=[], kb_b=[], rhs_b=[], decay=[], eg=[], gcol=[])
        for pp in pps:
            g_hi, g_mid, g_lo = _split3(g_all[rows(pp)])
            gcp = _mm(tri_b, g_hi) + _mm(tri_b, g_mid) + _mm(tri_b, g_lo)
            g["gc"][pp] = (gcp, gcp.T)
        for pp, h in g["items"]:
            gcp, gct = g["gc"][pp]
            q = qkv_ref[rows(pp), h * Dh:(h + 1) * Dh].astype(F32)
            k = qkv_ref[rows(pp), D_DN + h * Dh:D_DN + (h + 1) * Dh].astype(F32)
            v = qkv_ref[rows(pp), 2 * D_DN + h * Dh:2 * D_DN + (h + 1) * Dh].astype(F32)
            q = q * (lax.rsqrt(jnp.sum(q * q, axis=-1, keepdims=True) + EPS) * (Dh ** -0.5))
            k = k * lax.rsqrt(jnp.sum(k * k, axis=-1, keepdims=True) + EPS)
            gcl = gcp[:, h:h + 1]
            grow = gct[h:h + 1, :]
            bcol = beta_all[rows(pp), H + h:H + h + 1]
            diff = gcl - grow
            g["decay"].append(jnp.where(causal, jnp.exp(jnp.where(causal, diff, 0.0)), 0.0))
            e = jnp.exp(gcl)
            kb = k * bcol
            g["qn"].append(q)
            g["kn"].append(k)
            g["kb_b"].append(_bf(kb))
            g["rhs_b"].append(jnp.concatenate([_bf(kb * e), _bf(v * bcol)], axis=1))
            g["eg"].append(e)
            g["gcol"].append(gcl)
        return g

    def matmul_stages(g):
        n = range(len(g["items"]))

        def each(fn):
            return [fn(i) for i in n]

        def s_m():
            g["kn_b"] = each(lambda i: _bf(g["kn"][i]))
            g["m"] = each(lambda i: jnp.where(strict, _mm_nt(g["kb_b"][i], g["kn_b"][i]) * g["decay"][i], 0.0))

        def s_d2():
            g["d"] = each(lambda i: jnp.where(blk, g["m"][i], 0.0))
            g["l_b"] = each(lambda i: _bf(jnp.where(blk, 0.0, g["m"][i])))
            g["d_b"] = each(lambda i: _bf(g["d"][i]))
            g["d2"] = each(lambda i: _mm(g["d_b"][i], g["d_b"][i]))

        def s_d4():
            g["d2_b"] = each(lambda i: _bf(g["d2"][i]))
            g["d4"] = each(lambda i: _mm(g["d2_b"][i], g["d2_b"][i]))
            g["p1"] = each(lambda i: _mm(_bf(eye - g["d"][i]), _bf(eye + g["d2"][i])))

        def s_d8():
            g["d4_b"] = each(lambda i: _bf(g["d4"][i]))
            g["d8"] = each(lambda i: _mm(g["d4_b"][i], g["d4_b"][i]))
            g["p2"] = each(lambda i: _mm(_bf(g["p1"][i]), _bf(eye + g["d4"][i])))

        def s_invd():
            g["inv_d_b"] = each(lambda i: _bf(_mm(_bf(g["p2"][i]), _bf(eye + g["d8"][i]))))

        def s_n1():
            g["n1"] = each(lambda i: _mm(g["inv_d_b"][i], g["l_b"][i]))

        def s_n2():
            g["n1_b"] = each(lambda i: _bf(g["n1"][i]))
            g["n2"] = each(lambda i: _mm(g["n1_b"][i], g["n1_b"][i]))

        def s_invn():
            g["inv_n"] = each(lambda i: _mm(_bf(eye - g["n1"][i]), _bf(eye + g["n2"][i])))

        def s_tinv():
            g["tinv"] = each(lambda i: _mm(_bf(g["inv_n"][i]), g["inv_d_b"][i]))

        def s_sol():
            g["sol_b"] = each(lambda i: _bf(_mm(_bf(g["tinv"][i]), g["rhs_b"][i])))
            g["attn"] = each(lambda i: _mm_nt(_bf(g["qn"][i]), g["kn_b"][i]) * g["decay"][i])

        def s_av():
            g["av"] = each(lambda i: _mm(_bf(g["attn"][i]), g["sol_b"][i]))

        def s_post():
            g["qe"], g["oloc"], g["gb"], g["last"] = [], [], [], []
            for i, (pp, h) in enumerate(g["items"]):
                gcp = g["gc"][pp][0]
                g["qe"].append(g["qn"][i] * g["eg"][i] - g["av"][i][:, :Dh])
                g["oloc"].append(g["av"][i][:, Dh:])
                g_end0 = gcp[C - 1:C, h:h + 1]
                g_end1 = gcp[P - 1:P, h:h + 1]
                glast = jnp.where(first_row1, g_end0, g_end1)
                kdt = (g["kn"][i] * jnp.exp(glast - g["gcol"][i])).T
                sol_b = g["sol_b"][i]
                g["gb"].append((_mm(_bf(jnp.where(first_col, kdt, 0.0)), sol_b),
                                _mm(_bf(jnp.where(first_col, 0.0, kdt)), sol_b)))
                g["last"].append((jnp.exp(g_end0), jnp.exp(g_end1)))

        return [s_m, s_d2, s_d4, s_d8, s_invd, s_n1, s_n2, s_invn, s_tinv, s_sol, s_av, s_post]

    states = [state_ref[h] for h in range(H)]
    normg = normg_ref[...]

    def seq_steps(g, pps):
        def step(ip, pp, half):
            r0 = pp * P + half * C
            for h in range(H):
                i = ip * H + h
                g_mat = g["gb"][i][half]
                s_mat = states[h]
                lhs = jnp.concatenate([g["qe"][i][half * C:(half + 1) * C], g_mat[:, :Dh]], axis=0)
                r = _bdot(lhs, s_mat)
                o = r[:C] + g["oloc"][i][half * C:(half + 1) * C]
                states[h] = g["last"][i][half] * s_mat - r[C:] + g_mat[:, Dh:]
                gate = gdn_ref[r0:r0 + C, h * Dh:(h + 1) * Dh].astype(F32)
                o_ref[r0:r0 + C, h * Dh:(h + 1) * Dh] = (_rms(o, normg) * _silu(gate)).astype(BF16)
        return [functools.partial(step, ip, pp, half) for ip, pp in enumerate(pps) for half in range(2)]

    def save_states():
        for h in range(H):
            state_ref[h] = states[h]

    return prep, matmul_stages, seq_steps, save_states


def _merge_groups(groups):
    out = dict(items=[], gc={}, qn=[], kn=[], kb_b=[], rhs_b=[], decay=[], eg=[], gcol=[])
    for g in groups:
        out["gc"].update(g["gc"])
        for key in ("items", "qn", "kn", "kb_b", "rhs_b", "decay", "eg", "gcol"):
            out[key].extend(g[key])
    return out


def _front_kernel(x_ref, g_ref, w_ref, convw_ref, lng_ref, lnb_ref, wsp_ref, bias_ref, hpar_ref, normg_ref,
                  u0_ref, u1_ref, gssm_ref, ysg_ref, ydn_ref,
                  halo_ref, state_ref, qkv_scr, ab_scr, gdn_scr, *, ts):
    tm = ts

    @pl.when(pl.program_id(1) == 0)
    def _():
        halo_ref[...] = jnp.zeros_like(halo_ref)
        state_ref[...] = jnp.zeros_like(state_ref)

    h = _rms(x_ref[...], g_ref[...]).astype(BF16)

    def proj(off, width):
        return _mm(h, w_ref[:, off:off + width])

    def conv_block(x, c0):
        cols = slice(c0, c0 + CONV_BLK)
        w = convw_ref[:, cols]
        xe = jnp.concatenate([halo_ref[:, cols], x], axis=0)
        conv = (x * w[3:4] + xe[7:7 + tm] * w[2:3] + xe[6:6 + tm] * w[1:2] + xe[5:5 + tm] * w[0:1])
        halo_ref[:, cols] = x[tm - 8:tm]
        qkv_scr[:, cols] = _silu(conv).astype(BF16)

    head_of_lane = jnp.right_shift(lax.broadcasted_iota(jnp.int32, (1, D_SG), 1), 6)
    sg = {}

    def sg_proj():
        sg["z"] = proj(OFF_SG, 3 * D_SG)

    def sg_norm():
        v = _gelu(sg["z"][:, D_SG:2 * D_SG])
        mu = jnp.mean(v, axis=-1, keepdims=True)
        vc = v - mu
        sg["vn"] = vc * lax.rsqrt(jnp.mean(vc * vc, axis=-1, keepdims=True) + EPS) * lng_ref[...] + lnb_ref[...]

    def sg_gate():
        sg["ug"] = _gelu(sg["z"][:, :D_SG]) * _silu(sg["z"][:, 2 * D_SG:])

    def sg_chunk(c):
        rows = slice(c * SG_CHUNK, (c + 1) * SG_CHUNK)
        vst = jnp.concatenate([_bf(jnp.where(head_of_lane == hh, sg["vn"][rows], 0.0)) for hh in range(SG_HEADS)],
                              axis=0)
        s = _mm(wsp_ref[...], vst) + bias_ref[...]
        ysg_ref[rows, :] = (sg["ug"][rows] * s).astype(BF16)

    def plain_u():
        u_ssm = proj(OFF_U, D_SSM)
        u0_ref[...] = u_ssm[:, :LANES]
        u1_ref[...] = u_ssm[:, LANES:]

    def plain_gssm():
        gssm_ref[...] = proj(OFF_GSSM, D_SSM).astype(BF16)

    def plain_gdn(c):
        gdn_scr[:, c * 256:(c + 1) * 256] = proj(OFF_GDN + c * 256, 256).astype(BF16)

    ab_scr[...] = proj(OFF_AB, AB_PAD)
    nblk = W_QKV // CONV_BLK
    x_next = proj(OFF_QKV, CONV_BLK)
    for kb in range(nblk):
        x_cur = x_next
        if kb + 1 < nblk:
            x_next = proj(OFF_QKV + (kb + 1) * CONV_BLK, CONV_BLK)
        else:
            sg_proj()
        conv_block(x_cur, kb * CONV_BLK)

    prep, matmul_stages, seq_steps, save_states = _dnet_tasks(
        qkv_scr, ab_scr, gdn_scr, hpar_ref, normg_ref, ydn_ref, state_ref, ts)
    npair = ts // DN_PAIR
    mxu_tasks = [plain_u, plain_gssm, lambda: plain_gdn(0), lambda: plain_gdn(1)]
    groups = []
    for pp in range(npair):
        if mxu_tasks:
            mxu_tasks.pop(0)()
        groups.append(prep([pp]))
    for task in mxu_tasks:
        task()
    g = _merge_groups(groups)

    vpu_tasks = [sg_norm, sg_gate] + [functools.partial(sg_chunk, c) for c in range(tm // SG_CHUNK)]
    for stage in matmul_stages(g):
        stage()
        if vpu_tasks:
            vpu_tasks.pop(0)()
    for task in vpu_tasks:
        task()

    for step in seq_steps(g, list(range(npair))):
        step()
    save_states()


def _front(x3, layer, norm_g, w_in_p, conv_w, ln_g, ln_b, wsp, bias, hpar, dn_norm_g, ts):
    bsz, seq, _ = x3.shape
    tile = lambda w: pl.BlockSpec((None, ts, w), lambda b, j: (b, j, 0))
    lay2 = lambda a, c: pl.BlockSpec((None, a, c), lambda b, j: (layer, 0, 0))
    sds = lambda w, dt: jax.ShapeDtypeStruct((bsz, seq, w), dt)
    return pl.pallas_call(
        functools.partial(_front_kernel, ts=ts),
        out_shape=(sds(LANES, F32), sds(LANES, F32), sds(D_SSM, BF16), sds(D_SG, BF16), sds(D_DN, BF16)),
        grid=(bsz, seq // ts),
        in_specs=[tile(D_MODEL), lay2(1, D_MODEL), lay2(D_MODEL, D_IN_PAD), lay2(DN_CONV, W_QKV),
                  lay2(1, D_SG), lay2(1, D_SG), lay2(SG_CHUNK, SG_HEADS * SG_CHUNK), lay2(SG_CHUNK, D_SG),
                  lay2(8, AB_PAD), lay2(1, DN_HEAD_DIM)],
        out_specs=(tile(LANES), tile(LANES), tile(D_SSM), tile(D_SG), tile(D_DN)),
        scratch_shapes=[pltpu.VMEM((8, W_QKV), F32),
                        pltpu.VMEM((DN_HEADS, DN_HEAD_DIM, DN_HEAD_DIM), F32),
                        pltpu.VMEM((ts, W_QKV), BF16),
                        pltpu.VMEM((ts, AB_PAD), F32),
                        pltpu.VMEM((ts, D_DN), BF16)],
        compiler_params=_cparams(("parallel", "arbitrary")),
        name="front",
    )(x3, norm_g, w_in_p, conv_w, ln_g, ln_b, wsp, bias, hpar, dn_norm_g)


_S5_OFF = (np.arange(S5_T)[None, :] - np.arange(SSM_GROUPS)[:, None]) % S5_T
_S5_TAU = _S5_OFF[:, None, :] - _S5_OFF[:, :, None]


def _s5_matrices(a_re, a_im, b_re, b_im, c_re, c_im, d_skip, log_step):
    f32 = F32
    G, N, Cg, T, W, GP = SSM_GROUPS, SSM_STATE, SSM_GROUP, S5_T, S5_W, S5_PAIRS
    hp = lax.Precision.HIGHEST
    step = jnp.exp(log_step.astype(f32))[:, None]
    ar, ai = a_re.astype(f32), a_im.astype(f32)
    wr, wi = ar * step, ai * step

    def lam_pow(e):
        e = jnp.asarray(e, f32)[..., None]
        shp = (G,) + (1,) * (e.ndim - 2) + (N,)
        mag = jnp.exp(e * wr.reshape(shp))
        return mag * jnp.cos(e * wi.reshape(shp)), mag * jnp.sin(e * wi.reshape(shp))

    lam_re, lam_im = lam_pow(np.ones((G,), np.float32))
    den = ar * ar + ai * ai
    nr, ni = lam_re - 1.0, lam_im
    f_re = (nr * ar + ni * ai) / den
    f_im = (ni * ar - nr * ai) / den
    br, bi = b_re.astype(f32), b_im.astype(f32)
    bb_re = f_re[..., None] * br - f_im[..., None] * bi
    bb_im = f_re[..., None] * bi + f_im[..., None] * br
    cr, ci = c_re.astype(f32), c_im.astype(f32)
    valid = (_S5_TAU >= 0)
    pr, pi = lam_pow(np.where(valid, _S5_TAU, 0))
    pr = jnp.where(valid[..., None], pr, 0.0).reshape(G, T * T, N)
    pi = jnp.where(valid[..., None], pi, 0.0).reshape(G, T * T, N)
    crt = jnp.transpose(cr, (0, 2, 1))
    cit = jnp.transpose(ci, (0, 2, 1))
    a_mat = (crt[:, :, :, None] * bb_re[:, :, None, :] - cit[:, :, :, None] * bb_im[:, :, None, :]).reshape(G, N, Cg * Cg)
    b_mat = (cit[:, :, :, None] * bb_re[:, :, None, :] + crt[:, :, :, None] * bb_im[:, :, None, :]).reshape(G, N, Cg * Cg)
    m4 = (jnp.einsum('gxn,gny->gxy', pr, a_mat, precision=hp)
          - jnp.einsum('gxn,gny->gxy', pi, b_mat, precision=hp))
    m5 = jnp.transpose(m4.reshape(G, T, T, Cg, Cg), (0, 1, 4, 2, 3))
    skip = (jnp.eye(T, dtype=f32)[None, :, None, :, None] * jnp.eye(Cg, dtype=f32)[None, None, :, None, :]
            * d_skip.astype(f32)[:, None, None, None, :])
    m_mat = (m5 + skip).reshape(G, W, W)
    qr, qi = lam_pow(T - 1 - _S5_OFF)
    bbt_re = jnp.transpose(bb_re, (0, 2, 1))
    bbt_im = jnp.transpose(bb_im, (0, 2, 1))
    bm_re = (qr[:, :, None, :] * bbt_re[:, None] - qi[:, :, None, :] * bbt_im[:, None]).reshape(G, W, N)
    bm_im = (qr[:, :, None, :] * bbt_im[:, None] + qi[:, :, None, :] * bbt_re[:, None]).reshape(G, W, N)
    er, ei = lam_pow(_S5_OFF + 1)
    ert = jnp.transpose(er, (0, 2, 1))[:, :, :, None]
    eit = jnp.transpose(ei, (0, 2, 1))[:, :, :, None]
    cm_re = (crt[:, :, None, :] * ert - cit[:, :, None, :] * eit).reshape(G, N, W)
    cm_im = (-crt[:, :, None, :] * eit - cit[:, :, None, :] * ert).reshape(G, N, W)
    z_b = jnp.zeros((GP, W, N), f32)
    bm_r = bm_re.reshape(GP, 2, W, N)
    bm_i = bm_im.reshape(GP, 2, W, N)
    bm_pair = jnp.concatenate([
        jnp.concatenate([bm_r[:, 0], z_b, bm_i[:, 0], z_b], axis=2),
        jnp.concatenate([z_b, bm_r[:, 1], z_b, bm_i[:, 1]], axis=2)], axis=1)
    z_c = jnp.zeros((GP, N, W), f32)
    cm_r = cm_re.reshape(GP, 2, N, W)
    cm_i = cm_im.reshape(GP, 2, N, W)
    cm_pair = jnp.concatenate([
        jnp.concatenate([cm_r[:, 0], z_c], axis=2),
        jnp.concatenate([z_c, cm_r[:, 1]], axis=2),
        jnp.concatenate([cm_i[:, 0], z_c], axis=2),
        jnp.concatenate([z_c, cm_i[:, 1]], axis=2)], axis=1)
    tr, ti = lam_pow(np.tile(T * np.arange(S5_TAB, dtype=np.float32)[None, :], (G, 1)))
    lam_t = jnp.stack([jnp.transpose(tr, (1, 0, 2)).reshape(S5_TAB, GP * 2 * N),
                       jnp.transpose(ti, (1, 0, 2)).reshape(S5_TAB, GP * 2 * N)], axis=0)
    return m_mat, bm_pair, cm_pair, lam_t


def _s5_kernel(u0_ref, u1_ref, bm_ref, mm_ref, cm_ref, lam_ref, y0_ref, y1_ref,
               h_scr, x_scr, hs_scr, *, ts):
    T, G, Cg, W, GP = S5_T, SSM_GROUPS, SSM_GROUP, S5_W, S5_PAIRS
    J = ts // T

    @pl.when(pl.program_id(1) == 0)
    def _():
        h_scr[...] = jnp.zeros_like(h_scr)

    lane_blk = jnp.right_shift(lax.broadcasted_iota(jnp.int32, (1, W), 1), 4)
    bit_set = [jnp.bitwise_and(lane_blk, 1 << s) != 0 for s in range(4)]

    def rotate_by_block(src):
        cur = list(src)
        for s in range(4):
            cur = [jnp.where(bit_set[s], cur[(j + (1 << s)) % T], cur[j]) for j in range(T)]
        return cur

    def pick_all(src):
        rot = rotate_by_block(src)
        return [rot[(-f) % T] for f in range(T)]

    a = []
    for t in range(T):
        u_t = jnp.concatenate([u0_ref[pl.ds(t, J, stride=T), :], u1_ref[pl.ds(t, J, stride=T), :]], axis=1)
        a.append(u_t if t == 0 else pltpu.roll(u_t, Cg * t, axis=1))
    ug = pick_all(a)

    for p in range(GP):
        lhs = jnp.concatenate([ug[2 * p], ug[2 * p + 1]], axis=1)
        xl = _mm(_bf(lhs), bm_ref[p])
        x_scr[:, :, 256 * p:256 * (p + 1)] = xl.reshape(J // 8, 8, 256)

    tab_re = lam_ref[0]
    tab_im = lam_ref[1]
    sub = lax.broadcasted_iota(jnp.int32, (8, LANES), 0)

    def shifted(v, d):
        return jnp.where(sub >= d, pltpu.roll(v, d, axis=0), 0.0)

    def body(jo, h):
        x = x_scr[jo]
        parts = []
        for p in range(GP):
            re_l = slice(256 * p, 256 * p + 128)
            im_l = slice(256 * p + 128, 256 * p + 256)
            tl = slice(128 * p, 128 * p + 128)
            xr, xi = x[:, re_l], x[:, im_l]
            for d in (1, 2, 4):
                ar, ai = tab_re[d:d + 1, tl], tab_im[d:d + 1, tl]
                sr, si = shifted(xr, d), shifted(xi, d)
                xr, xi = xr + (ar * sr - ai * si), xi + (ar * si + ai * sr)
            hr, hi = h[:, re_l], h[:, im_l]
            pr, pi = tab_re[0:8, tl], tab_im[0:8, tl]
            hs_scr[jo, :, re_l] = pr * hr - pi * hi + shifted(xr, 1)
            hs_scr[jo, :, im_l] = pr * hi + pi * hr + shifted(xi, 1)
            er, ei = tab_re[8:9, tl], tab_im[8:9, tl]
            parts.append(er * hr - ei * hi + xr[7:8])
            parts.append(er * hi + ei * hr + xi[7:8])
        return jnp.concatenate(parts, axis=1)

    h_scr[...] = lax.fori_loop(0, J // 8, body, h_scr[...])

    hs = hs_scr[...].reshape(J, S5_LANES)
    yg = []
    for p in range(GP):
        yh = _mm(_bf(hs[:, 256 * p:256 * (p + 1)]), cm_ref[p])
        for i in range(2):
            g = 2 * p + i
            yg.append(_mm(_bf(ug[g]), mm_ref[g]) + yh[:, i * W:(i + 1) * W])
    yt = pick_all(yg)
    for t in range(T):
        y_t = yt[t]
        if t:
            y_t = pltpu.roll(y_t, W - Cg * t, axis=1)
        y0_ref[pl.ds(t, J, stride=T), :] = y_t[:, :LANES]
        y1_ref[pl.ds(t, J, stride=T), :] = y_t[:, LANES:]


def _s5(u0, u1, layer, mats, ts):
    bsz, seq, _ = u0.shape
    bm, mm, cm, lam_t = mats
    G, W, GP = SSM_GROUPS, S5_W, S5_PAIRS
    J = ts // S5_T
    lay3 = lambda shape: pl.BlockSpec((None,) + shape, lambda b, j: (layer, 0, 0, 0))
    half = pl.BlockSpec((None, ts, LANES), lambda b, j: (b, j, 0))
    return pl.pallas_call(
        functools.partial(_s5_kernel, ts=ts),
        out_shape=(jax.ShapeDtypeStruct((bsz, seq, LANES), F32),) * 2,
        grid=(bsz, seq // ts),
        in_specs=[half, half,
                  lay3((GP, 2 * W, 256)), lay3((G, W, W)), lay3((GP, 256, 2 * W)),
                  pl.BlockSpec((None, 2, S5_TAB, GP * 128), lambda b, j: (layer, 0, 0, 0))],
        out_specs=(half, half),
        scratch_shapes=[pltpu.VMEM((1, S5_LANES), F32),
                        pltpu.VMEM((J // 8, 8, S5_LANES), F32),
                        pltpu.VMEM((J // 8, 8, S5_LANES), F32)],
        compiler_params=_cparams(("parallel", "arbitrary")),
        name="s5",
    )(u0, u1, bm, mm, cm, lam_t)


def _outproj_kernel(x_ref, y0_ref, y1_ref, gssm_ref, ydn_ref, ysg_ref, p_ref, wglu_ref, bglu_ref,
                    wout_ref, pleg_ref, wgate_ref, wple_ref, fng_ref, o_ref, *, final, tm):
    halves = [slice(0, tm // 2), slice(tm // 2, tm)]
    ple = [_bdot(p_ref[r, :], wple_ref[...]) for r in halves]
    y = [_gelu(jnp.concatenate([y0_ref[r, :], y1_ref[r, :]], axis=1)) for r in halves]
    glu = [_bdot(yy, wglu_ref[...]) for yy in y]
    y = [yy * _sigmoid(gg + bglu_ref[...]) * _silu(gssm_ref[r, :].astype(F32)) for yy, gg, r in zip(y, glu, halves)]
    ycat = [jnp.concatenate([yy.astype(BF16), ydn_ref[r, :], ysg_ref[r, :]], axis=1) for yy, r in zip(y, halves)]
    x1 = [x_ref[r, :] + _mm(yc, wout_ref[...]) for yc, r in zip(ycat, halves)]
    hn = [_bf(_rms(xx, pleg_ref[...])) for xx in x1]
    gate = [_sigmoid(_mm(hh, wgate_ref[...])) for hh in hn]
    for r, xx, gg, pp in zip(halves, x1, gate, ple):
        x2 = xx + gg * pp
        if final:
            x2 = _rms(x2, fng_ref[...])
        o_ref[r, :] = x2


def _outproj(x2, y0, y1, gssm, ydn, ysg, p3, layer, w_glu, b_glu, w_out, ple_g, w_gate, w_ple, fn_g, tm, final):
    n = x2.shape[0]
    row = lambda w: pl.BlockSpec((tm, w), lambda i: (i, 0))
    lay2 = lambda a, b: pl.BlockSpec((None, a, b), lambda i: (layer, 0, 0))
    return pl.pallas_call(
        functools.partial(_outproj_kernel, final=final, tm=tm),
        out_shape=jax.ShapeDtypeStruct((n, D_MODEL), F32),
        grid=(n // tm,),
        in_specs=[row(D_MODEL), row(LANES), row(LANES), row(D_SSM), row(D_DN), row(D_SG),
                  pl.BlockSpec((None, tm, D_PLE), lambda i: (layer, i, 0)),
                  lay2(D_SSM, D_SSM), lay2(1, D_SSM), lay2(D_MODEL, D_MODEL), lay2(1, D_MODEL),
                  lay2(D_MODEL, D_MODEL), lay2(D_PLE, D_MODEL),
                  pl.BlockSpec((1, D_MODEL), lambda i: (0, 0))],
        out_specs=row(D_MODEL),
        compiler_params=_cparams(("parallel",)),
        name="outproj",
    )(x2, y0, y1, gssm, ydn, ysg, p3, w_glu, b_glu, w_out, ple_g, w_gate, w_ple, fn_g)


def _permute_w_in(w):
    qkv_end = 2 * D_SSM + 3 * D_DN
    ab_end = qkv_end + 2 * DN_HEADS
    pad = jnp.zeros(w.shape[:-1] + (AB_PAD - 2 * DN_HEADS,), w.dtype)
    return jnp.concatenate([w[..., :qkv_end], w[..., ab_end:], w[..., qkv_end:ab_end], pad], axis=-1)


def _forward(x, p, norm_g, w_in, ssm_a_re, ssm_a_im, ssm_b_re, ssm_b_im, ssm_c_re, ssm_c_im,
             ssm_d, ssm_log_step, ssm_w_glu, ssm_b_glu, dn_conv_w, dn_a_log, dn_dt_bias, dn_norm_g,
             sg_ln_g, sg_ln_b, sg_w, sg_b, w_out, ple_norm_g, w_ple_gate, w_ple, final_norm_g,
             *, tm, ts, ts_s5):
    bsz, seq, _ = x.shape
    depth = w_in.shape[0]
    n = bsz * seq
    p3 = p.reshape(depth, n, D_PLE)

    w_in_p = _permute_w_in(w_in.astype(BF16))
    causal = jnp.tril(jnp.ones((SG_CHUNK, SG_CHUNK), dtype=bool))
    wsp = jnp.transpose(jnp.where(causal, sg_w, 0.0), (0, 2, 1, 3)).reshape(depth, SG_CHUNK, SG_HEADS * SG_CHUNK)
    wsp = wsp.astype(BF16)
    sg_bias = jnp.repeat(jnp.transpose(sg_b, (0, 2, 1)), SG_HEAD_DIM, axis=2)
    hpar = jnp.zeros((depth, 8, AB_PAD), F32)
    hpar = hpar.at[:, 0, :DN_HEADS].set(dn_a_log.astype(F32)).at[:, 1, :DN_HEADS].set(dn_dt_bias.astype(F32))
    m_mat, bm_pair, cm_pair, lam_t = jax.vmap(_s5_matrices)(
        ssm_a_re, ssm_a_im, ssm_b_re, ssm_b_im, ssm_c_re, ssm_c_im, ssm_d, ssm_log_step)
    s5_mats = (bm_pair.astype(BF16), m_mat.astype(BF16), cm_pair.astype(BF16), lam_t)
    r3 = lambda a: a.reshape(depth, 1, a.shape[-1])
    w_glu_b, w_out_b, w_gate_b, w_ple_b = (ssm_w_glu.astype(BF16), w_out.astype(BF16),
                                          w_ple_gate.astype(BF16), w_ple.astype(BF16))

    x3 = x
    for i in range(depth):
        u0, u1, g_ssm, y_sg, y_dn = _front(x3, i, r3(norm_g), w_in_p, dn_conv_w, r3(sg_ln_g), r3(sg_ln_b), wsp, sg_bias,
                                           hpar, r3(dn_norm_g), ts)
        y0, y1 = _s5(u0, u1, i, s5_mats, ts_s5)
        flat = lambda a: a.reshape(n, a.shape[-1])
        x2 = _outproj(flat(x3), flat(y0), flat(y1), flat(g_ssm), flat(y_dn), flat(y_sg), p3, i,
                      w_glu_b, r3(ssm_b_glu), w_out_b, r3(ple_norm_g), w_gate_b, w_ple_b,
                      final_norm_g.reshape(1, D_MODEL), tm, final=(i == depth - 1))
        x3 = x2.reshape(bsz, seq, D_MODEL)
    return x3


def kernel(x, p, norm_g, w_in, ssm_a_re, ssm_a_im, ssm_b_re, ssm_b_im, ssm_c_re, ssm_c_im, ssm_d, ssm_log_step, ssm_w_glu, ssm_b_glu, dn_conv_w, dn_a_log, dn_dt_bias, dn_norm_g, sg_ln_g, sg_ln_b, sg_w, sg_b, w_out, ple_norm_g, w_ple_gate, w_ple, final_norm_g):
    seq = x.shape[1]
    return _forward(x, p, norm_g, w_in, ssm_a_re, ssm_a_im, ssm_b_re, ssm_b_im, ssm_c_re, ssm_c_im,
                    ssm_d, ssm_log_step, ssm_w_glu, ssm_b_glu, dn_conv_w, dn_a_log, dn_dt_bias, dn_norm_g,
                    sg_ln_g, sg_ln_b, sg_w, sg_b, w_out, ple_norm_g, w_ple_gate, w_ple, final_norm_g,
                    tm=512, ts=min(512, seq), ts_s5=min(2048, seq))
```
